```python
import math
import jax, jax.numpy as jnp
from jax import lax
import numpy as np

D_MODEL = 1024
BATCH = 16
SEQ = 256
DEPTH = 2
DEC_BATCH = 8
DEC_SEQ = 4096
PAST_LEN = 256

GRID_W = 64
N_EVEN = (DEPTH + 1) // 2
N_ODD = DEPTH // 2
FOURIER_GROUPS = 4
FOURIER_WIDTH = D_MODEL // 2
FOURIER_GROUP_DIM = FOURIER_WIDTH // FOURIER_GROUPS
DIFF_HEADS = 4
DIFF_HEAD_DIM = D_MODEL // (4 * DIFF_HEADS)
DIFF_WIDTH = DIFF_HEADS * 2 * DIFF_HEAD_DIM
IN_AB_WIDTH = FOURIER_WIDTH + 3 * DIFF_WIDTH
OUT_AB_WIDTH = FOURIER_WIDTH + DIFF_WIDTH
ROPE_BASE = 10000.0
ROPE_AXIS_DIM = DIFF_HEAD_DIM // 2
Q_BLOCK = 128
CONV_WIDTH = 31
CONV_DIM = D_MODEL
D_FF_DENSE = 2816
N_EXPERTS = 8
TOP_K = 2
D_FF_EXPERT = 3584
NORM_EPS = 1e-6
SUBLN_EPS = 1e-5

kernel_name = "hybrid_fourier_diffattn_conformer_dit_step"


def rmsnorm(x, g, eps=NORM_EPS):
    xf = x.astype(jnp.float32)
    y = xf * lax.rsqrt(jnp.mean(xf * xf, axis=-1, keepdims=True) + eps)
    return (y * g.astype(jnp.float32)).astype(x.dtype)


def layernorm(x, g, b, eps=NORM_EPS):
    xf = x.astype(jnp.float32)
    mu = jnp.mean(xf, axis=-1, keepdims=True)
    var = jnp.mean(jnp.square(xf - mu), axis=-1, keepdims=True)
    y = (xf - mu) * lax.rsqrt(var + eps)
    return (y * g.astype(jnp.float32) + b.astype(jnp.float32)).astype(x.dtype)


def modulation(cond, w_ada, b_ada):
    m = jax.nn.silu(cond) @ w_ada + b_ada
    return jnp.split(m[:, None, :], 6, axis=-1)


def modulate(h, shift, scale):
    return h * (1.0 + scale) + shift


def lambda_init(layer):
    return 0.8 - 0.6 * math.exp(-0.3 * layer)


def axial_angles(n_tokens):
    rows = n_tokens // GRID_W
    row = jnp.repeat(jnp.arange(rows, dtype=jnp.float32), GRID_W)
    col = jnp.tile(jnp.arange(GRID_W, dtype=jnp.float32), rows)
    inv = jnp.power(ROPE_BASE, -jnp.arange(ROPE_AXIS_DIM // 2, dtype=jnp.float32) / (ROPE_AXIS_DIM // 2))
    return row[:, None] * inv, col[:, None] * inv


def rotate(x, ang):
    cos = jnp.cos(ang)[None, :, None, None, :]
    sin = jnp.sin(ang)[None, :, None, None, :]
    x1, x2 = jnp.split(x, 2, axis=-1)
    return jnp.concatenate([x1 * cos - x2 * sin, x2 * cos + x1 * sin], axis=-1)


def apply_axial_rope(x, ang_row, ang_col):
    xf = x.astype(jnp.float32)
    xr, xc = jnp.split(xf, 2, axis=-1)
    return jnp.concatenate([rotate(xr, ang_row), rotate(xc, ang_col)], axis=-1).astype(x.dtype)


def project_ab(h, w_in):
    b, l = h.shape[:2]
    p = h @ w_in
    f = p[..., :FOURIER_WIDTH]
    q = p[..., FOURIER_WIDTH:FOURIER_WIDTH + DIFF_WIDTH].reshape(b, l, DIFF_HEADS, 2, DIFF_HEAD_DIM)
    k = p[..., FOURIER_WIDTH + DIFF_WIDTH:FOURIER_WIDTH + 2 * DIFF_WIDTH].reshape(b, l, DIFF_HEADS, 2, DIFF_HEAD_DIM)
    v = p[..., FOURIER_WIDTH + 2 * DIFF_WIDTH:].reshape(b, l, DIFF_HEADS, 2 * DIFF_HEAD_DIM)
    return f, q, k, v


def fourier_mix(f):
    b, l = f.shape[:2]
    g = f.reshape(b, l, FOURIER_GROUPS, FOURIER_GROUP_DIM).astype(jnp.float32)
    y = jnp.fft.fft2(g, axes=(1, 3), norm="ortho").real
    return y.reshape(b, l, FOURIER_WIDTH).astype(f.dtype)


def diff_lambda_value(lam_p, lam_init):
    lp = lam_p.astype(jnp.float32)
    return jnp.exp(jnp.sum(lp[0] * lp[1])) - jnp.exp(jnp.sum(lp[2] * lp[3])) + lam_init


def diff_attention(q, k, v, lam, subln_g, lam_init):
    b, lq = q.shape[:2]
    nb = lq // Q_BLOCK
    qb = q.reshape(b, nb, Q_BLOCK, DIFF_HEADS, 2, DIFF_HEAD_DIM).transpose(1, 0, 2, 3, 4, 5)
    scale = DIFF_HEAD_DIM ** -0.5

    def block(qblk):
        s = jnp.einsum('bqhcd,bkhcd->bhcqk', qblk, k, preferred_element_type=jnp.float32) * scale
        p = jax.nn.softmax(s, axis=-1)
        a = p[:, :, 0] - lam * p[:, :, 1]
        return jnp.einsum('bhqk,bkhe->bqhe', a.astype(v.dtype), v, preferred_element_type=jnp.float32)

    o = lax.map(block, qb)
    o = o.transpose(1, 0, 2, 3, 4).reshape(b, lq, DIFF_HEADS, 2 * DIFF_HEAD_DIM)
    o = rmsnorm(o, subln_g, SUBLN_EPS) * (1.0 - lam_init)
    return o.reshape(b, lq, DIFF_WIDTH).astype(q.dtype)


def conformer_conv(h, w_pw1, w_dw, b_dw, ln_g, ln_b, w_pw2):
    u = h @ w_pw1
    a, g = jnp.split(u, 2, axis=-1)
    u = a * jax.nn.sigmoid(g)
    u = lax.conv_general_dilated(
        u, w_dw[:, None, :], window_strides=(1,),
        padding=[(CONV_WIDTH // 2, CONV_WIDTH // 2)],
        dimension_numbers=('NWC', 'WIO', 'NWC'),
        feature_group_count=CONV_DIM) + b_dw
    u = jax.nn.silu(layernorm(u, ln_g, ln_b))
    return u @ w_pw2


def swiglu(h, w1, w3, w2):
    return (jax.nn.silu(h @ w1) * (h @ w3)) @ w2


def moe_swiglu(h, w_router, w1, w3, w2):
    b, l, d = h.shape
    x = h.reshape(b * l, d)
    logits = (x @ w_router).astype(jnp.float32)
    top_v, top_i = lax.top_k(logits, TOP_K)
    gates = jax.nn.softmax(top_v, axis=-1)
    dense_gate = jnp.sum(jax.nn.one_hot(top_i, N_EXPERTS, dtype=jnp.float32) * gates[..., None], axis=1)
    y = jnp.zeros(x.shape, jnp.float32)
    for e in range(N_EXPERTS):
        y = y + dense_gate[:, e:e + 1] * swiglu(x, w1[e], w3[e], w2[e]).astype(jnp.float32)
    return y.reshape(b, l, d).astype(h.dtype)


def setup_inputs(seed: int = 0) -> dict:
    key = jax.random.key(seed)
    ks = jax.random.split(key, 32)

    def nrm(k, shape, scale):
        return jax.random.normal(k, shape, jnp.float32) * scale

    kv_shape = (DEC_BATCH, N_EVEN, PAST_LEN, DIFF_HEADS, 2 * DIFF_HEAD_DIM)
    return {
        "x_prompt": nrm(ks[0], (BATCH, SEQ, D_MODEL), 1.0),
        "x_sample": nrm(ks[1], (DEC_BATCH, DEC_SEQ, D_MODEL), 1.0),
        "cache_k": nrm(ks[2], kv_shape, 1.0),
        "cache_v": nrm(ks[3], kv_shape, 1.0),
        "c": nrm(ks[4], (DEC_BATCH, D_MODEL), 1.0),
        "c_ctx": nrm(ks[5], (D_MODEL,), 1.0),
        "w_ada": nrm(ks[6], (DEPTH, D_MODEL, 6 * D_MODEL), 0.5 * D_MODEL ** -0.5),
        "b_ada": nrm(ks[7], (DEPTH, 6 * D_MODEL), 0.02),
        "norm_g": 1.0 + nrm(ks[8], (DEPTH, 2, D_MODEL), 0.02),
        "final_g": 1.0 + nrm(ks[9], (D_MODEL,), 0.02),
        "w_in_ab": nrm(ks[10], (N_EVEN, D_MODEL, IN_AB_WIDTH), D_MODEL ** -0.5),
        "w_out_ab": nrm(ks[11], (N_EVEN, OUT_AB_WIDTH, D_MODEL), OUT_AB_WIDTH ** -0.5),
        "diff_lambda": nrm(ks[12], (N_EVEN, 4, DIFF_HEAD_DIM), 0.1),
        "diff_subln": 1.0 + nrm(ks[13], (N_EVEN, 2 * DIFF_HEAD_DIM), 0.02),
        "conv_w_pw1": nrm(ks[14], (N_ODD, D_MODEL, 2 * CONV_DIM), D_MODEL ** -0.5),
        "conv_w_dw": nrm(ks[15], (N_ODD, CONV_WIDTH, CONV_DIM), CONV_WIDTH ** -0.5),
        "conv_b_dw": nrm(ks[16], (N_ODD, CONV_DIM), 0.02),
        "conv_ln_g": 1.0 + nrm(ks[17], (N_ODD, CONV_DIM), 0.02),
        "conv_ln_b": nrm(ks[18], (N_ODD, CONV_DIM), 0.02),
        "conv_w_pw2": nrm(ks[19], (N_ODD, CONV_DIM, D_MODEL), CONV_DIM ** -0.5),
        "ffn_w1": nrm(ks[20], (N_EVEN, D_MODEL, D_FF_DENSE), D_MODEL ** -0.5),
        "ffn_w3": nrm(ks[21], (N_EVEN, D_MODEL, D_FF_DENSE), D_MODEL ** -0.5),
        "ffn_w2": nrm(ks[22], (N_EVEN, D_FF_DENSE, D_MODEL), D_FF_DENSE ** -0.5),
        "moe_router": nrm(ks[23], (N_ODD, D_MODEL, N_EXPERTS), D_MODEL ** -0.5),
        "moe_w1": nrm(ks[24], (N_ODD, N_EXPERTS, D_MODEL, D_FF_EXPERT), D_MODEL ** -0.5),
        "moe_w3": nrm(ks[25], (N_ODD, N_EXPERTS, D_MODEL, D_FF_EXPERT), D_MODEL ** -0.5),
        "moe_w2": nrm(ks[26], (N_ODD, N_EXPERTS, D_FF_EXPERT, D_MODEL), D_FF_EXPERT ** -0.5),
    }


def reference(x_prompt, x_sample, cache_k, cache_v, c, c_ctx,
              w_ada, b_ada, norm_g, final_g,
              w_in_ab, w_out_ab, diff_lambda, diff_subln,
              conv_w_pw1, conv_w_dw, conv_b_dw, conv_ln_g, conv_ln_b, conv_w_pw2,
              ffn_w1, ffn_w3, ffn_w2,
              moe_router, moe_w1, moe_w3, moe_w2):
    bp, lp = x_prompt.shape[:2]
    bs, ls = x_sample.shape[:2]
    lc = cache_k.shape[2]
    ang_row, ang_col = axial_angles(ls)
    xp, xs = x_prompt, x_sample
    new_k, new_v = [], []
    for l in range(DEPTH):
        i = l // 2
        mp = modulation(c_ctx[None, :], w_ada[l], b_ada[l])
        ms = modulation(c, w_ada[l], b_ada[l])
        hp = modulate(rmsnorm(xp, norm_g[l, 0]), mp[0], mp[1])
        hs = modulate(rmsnorm(xs, norm_g[l, 0]), ms[0], ms[1])
        if l % 2 == 0:
            lam_init = lambda_init(l)
            lam = diff_lambda_value(diff_lambda[i], lam_init)
            fp, qp, kp, vp = project_ab(hp, w_in_ab[i])
            new_k.append(kp.reshape(bp, lp, DIFF_HEADS, 2 * DIFF_HEAD_DIM))
            new_v.append(vp)
            ap = diff_attention(qp, kp, vp, lam, diff_subln[i], lam_init)
            op = jnp.concatenate([fourier_mix(fp), ap], axis=-1) @ w_out_ab[i]
            fs, qs, ks_, vs = project_ab(hs, w_in_ab[i])
            qs = apply_axial_rope(qs, ang_row, ang_col)
            ks_ = apply_axial_rope(ks_, ang_row, ang_col)
            k_ctx = cache_k[:, i].reshape(bs, lc, DIFF_HEADS, 2, DIFF_HEAD_DIM).astype(ks_.dtype)
            k_all = jnp.concatenate([ks_, k_ctx], axis=1)
            v_all = jnp.concatenate([vs, cache_v[:, i].astype(vs.dtype)], axis=1)
            a_s = diff_attention(qs, k_all, v_all, lam, diff_subln[i], lam_init)
            os_ = jnp.concatenate([fourier_mix(fs), a_s], axis=-1) @ w_out_ab[i]
        else:
            op = conformer_conv(hp, conv_w_pw1[i], conv_w_dw[i], conv_b_dw[i],
                                conv_ln_g[i], conv_ln_b[i], conv_w_pw2[i])
            os_ = conformer_conv(hs, conv_w_pw1[i], conv_w_dw[i], conv_b_dw[i],
                                 conv_ln_g[i], conv_ln_b[i], conv_w_pw2[i])
        xp = xp + mp[2] * op
        xs = xs + ms[2] * os_
        hp = modulate(rmsnorm(xp, norm_g[l, 1]), mp[3], mp[4])
        hs = modulate(rmsnorm(xs, norm_g[l, 1]), ms[3], ms[4])
        if l % 2 == 0:
            fp_out = swiglu(hp, ffn_w1[i], ffn_w3[i], ffn_w2[i])
            fs_out = swiglu(hs, ffn_w1[i], ffn_w3[i], ffn_w2[i])
        else:
            fp_out = moe_swiglu(hp, moe_router[i], moe_w1[i], moe_w3[i], moe_w2[i])
            fs_out = moe_swiglu(hs, moe_router[i], moe_w1[i], moe_w3[i], moe_w2[i])
        xp = xp + mp[5] * fp_out
        xs = xs + ms[5] * fs_out
    y_prompt = rmsnorm(xp, final_g)
    y_sample = rmsnorm(xs, final_g)
    new_cache_k = jnp.stack(new_k, axis=1)
    new_cache_v = jnp.stack(new_v, axis=1)
    return (y_prompt, y_sample, new_cache_k, new_cache_v)
```

```python
import functools
import math

import numpy as np
import jax
import jax.numpy as jnp
from jax import lax
from jax.experimental import pallas as pl
from jax.experimental.pallas import tpu as pltpu

F32 = jnp.float32
BF16 = jnp.bfloat16

GRID_W = 64
ROPE_BASE = 10000.0
N_HEADS = 4
FOURIER_GROUP_DIM = 128
CONV_HALO = 16
TOP_K = 2
NORM_EPS = 1e-6
SUBLN_EPS = 1e-5
LANES = 128
SUBLANES = 8
LOG2E = 1.4426950408889634

TOKEN_TILE = 512
CONV_TILE = 256
ATTN_Q_TILE = 256
ATTN_K_TILE = 512
FOURIER_ROW_TILE = 256
MOE_TILE = 512
MOE_FF_CHUNK = 1792
VMEM_LIMIT = 56 * 1024 * 1024


def _cparams(semantics, vmem=VMEM_LIMIT):
    return pltpu.CompilerParams(dimension_semantics=semantics, vmem_limit_bytes=vmem)


def _const_spec(shape):
    nd = len(shape)
    return pl.BlockSpec(shape, lambda *_: (0,) * nd, pipeline_mode=pl.Buffered(1))


def _dot(a, b):
    return jnp.dot(a, b, preferred_element_type=F32)


def _silu(x):
    return x * jax.nn.sigmoid(x)


def _norm_mod(x, g, shift, scale):
    ms = jnp.mean(x * x, axis=-1, keepdims=True)
    return (x * lax.rsqrt(ms + NORM_EPS)) * (g * (1.0 + scale)) + shift


def _seg_index_map(tile, n_ctx, seq):
    def index_map(i, *_):
        start = i * tile
        return (jnp.where(start < n_ctx, 0, 1 + (start - n_ctx) // seq), 0, 0)
    return index_map


def _mod_kernel(c_ref, w_ref, b_ref, o_ref):
    c = c_ref[...]
    o_ref[0] = _dot(_silu(c).astype(BF16), w_ref[0].astype(BF16)) + b_ref[0]


def _modulation(cond, w_ada, b_ada):
    depth, d, n6 = w_ada.shape
    rows = cond.shape[0]
    tn = 1536
    out = pl.pallas_call(
        _mod_kernel,
        out_shape=jax.ShapeDtypeStruct((depth, rows, n6), F32),
        grid=(depth, n6 // tn),
        in_specs=[pl.BlockSpec((rows, d), lambda l, j: (0, 0)),
                  pl.BlockSpec((1, d, tn), lambda l, j: (l, 0, j)),
                  pl.BlockSpec((1, 1, tn), lambda l, j: (l, 0, j))],
        out_specs=pl.BlockSpec((1, rows, tn), lambda l, j: (l, 0, j)),
        compiler_params=_cparams(("parallel", "parallel")),
        name="modulation",
    )(cond, w_ada, b_ada.reshape(depth, 1, n6))
    return out.reshape(depth, rows, 6, d)


def _rope(t, cos, sin_lo, sin_hi):
    outs = []
    for h in range(t.shape[1] // LANES):
        th = t[:, h * LANES:(h + 1) * LANES]
        outs.append(th * cos + pltpu.roll(th, LANES - 16, 1) * sin_lo + pltpu.roll(th, 16, 1) * sin_hi)
    return jnp.concatenate(outs, axis=-1)


def _inproj_kernel(*refs, rope, cache, q_scale):
    x_ref, mod_ref, g_ref, w_ref, bd_ref = refs[:5]
    pos = 5
    if rope:
        cos_ref, slo_ref, shi_ref = refs[pos:pos + 3]
        pos += 3
    acas_ref, q_ref, k_ref, v_ref = refs[pos:pos + 4]
    m = mod_ref[0]
    h = _norm_mod(x_ref[...], g_ref[...], m[0:1], m[1:2]).astype(BF16)
    fw = bd_ref.shape[0]
    dw = q_ref.shape[1]
    f = _dot(h, w_ref[:, 0:fw])
    acas_ref[...] = _dot(f.astype(BF16), bd_ref[...]).astype(BF16)
    q = _dot(h, w_ref[:, fw:fw + dw])
    k = _dot(h, w_ref[:, fw + dw:fw + 2 * dw])
    v = _dot(h, w_ref[:, fw + 2 * dw:fw + 3 * dw])
    if cache:
        kc_ref, vc_ref = refs[pos + 4:pos + 6]
        kc_ref[...] = k
        vc_ref[...] = v
    if rope:
        cos, slo, shi = cos_ref[...], slo_ref[...], shi_ref[...]
        q = _rope(q, cos, slo, shi)
        k = _rope(k, cos, slo, shi)
    q_ref[...] = (q * q_scale).astype(BF16)
    k_ref[...] = k.astype(BF16)
    v_ref[...] = v.astype(BF16)


def _inproj(x, mod, g, w_in, bd, *, seq, mod_row0, rope_tables, cache, q_scale):
    n, d = x.shape
    tm = min(TOKEN_TILE, seq)
    fw = bd.shape[0]
    dw = (w_in.shape[1] - fw) // 3
    per_seq = seq // tm
    row = lambda i: (i, 0)
    in_specs = [pl.BlockSpec((tm, d), row),
                pl.BlockSpec((1, 6, d), lambda i: (mod_row0 + (i // per_seq if mod_row0 else 0), 0, 0)),
                _const_spec((1, d)), _const_spec(w_in.shape), _const_spec(bd.shape)]
    args = [x, mod, g, w_in, bd]
    if rope_tables is not None:
        in_specs += [pl.BlockSpec((tm, LANES), lambda i: (i % per_seq, 0))] * 3
        args += list(rope_tables)
    out_shape = [jax.ShapeDtypeStruct((n, 2 * fw), BF16)] + [jax.ShapeDtypeStruct((n, dw), BF16)] * 3
    out_specs = [pl.BlockSpec((tm, 2 * fw), row)] + [pl.BlockSpec((tm, dw), row)] * 3
    if cache:
        out_shape += [jax.ShapeDtypeStruct((n, dw), F32)] * 2
        out_specs += [pl.BlockSpec((tm, dw), row)] * 2
    return pl.pallas_call(
        functools.partial(_inproj_kernel, rope=rope_tables is not None, cache=cache, q_scale=q_scale),
        out_shape=out_shape, grid=(n // tm,), in_specs=in_specs, out_specs=out_specs,
        compiler_params=_cparams(("parallel",)),
        name="inproj_ctx" if cache else "inproj_lat",
    )(*args)


def _fourier_short_kernel(ct_ref, st_ref, acas_ref, o_ref, *, scale):
    fw = o_ref.shape[1]
    y = _dot(ct_ref[...], acas_ref[:, 0:fw]) - _dot(st_ref[...], acas_ref[:, fw:2 * fw])
    o_ref[...] = (y * scale).astype(BF16)


def _fourier_short(acas, seq):
    n, fw2 = acas.shape
    fw = fw2 // 2
    idx = (np.arange(seq)[:, None] * np.arange(seq)[None, :]) % seq
    ang = (2.0 * np.pi / seq) * idx
    ct = jnp.asarray(np.cos(ang), BF16)
    st = jnp.asarray(np.sin(ang), BF16)
    scale = 1.0 / math.sqrt(seq * FOURIER_GROUP_DIM)
    return pl.pallas_call(
        functools.partial(_fourier_short_kernel, scale=scale),
        out_shape=jax.ShapeDtypeStruct((n, fw), BF16),
        grid=(n // seq,),
        in_specs=[_const_spec((seq, seq)), _const_spec((seq, seq)),
                  pl.BlockSpec((seq, fw2), lambda b: (b, 0))],
        out_specs=pl.BlockSpec((seq, fw), lambda b: (b, 0)),
        compiler_params=_cparams(("parallel",)),
        name="fourier_ctx",
    )(ct, st, acas)


def _fourier_long_kernel(cb_ref, sb_ref, ca_ref, sa_ref, acas_ref, o_ref, ct_scr, st_scr, *, scale):
    @pl.when(pl.program_id(1) == 0)
    def _():
        ca, sa = ca_ref[0], sa_ref[0]
        cb, sb = cb_ref[...], sb_ref[...]
        ct_scr[...] = (ca * cb - sa * sb).astype(BF16)
        st_scr[...] = (sa * cb + ca * sb).astype(BF16)
    fw = o_ref.shape[1]
    y = _dot(ct_scr[...], acas_ref[:, 0:fw]) - _dot(st_scr[...], acas_ref[:, fw:2 * fw])
    o_ref[...] = (y * scale).astype(BF16)


def _fourier_long(acas, seq):
    n, fw2 = acas.shape
    fw = fw2 // 2
    nb = n // seq
    tr = FOURIER_ROW_TILE
    nr = seq // tr
    col = np.arange(seq)[None, :]
    beta = (2.0 * np.pi / seq) * ((np.arange(tr)[:, None] * col) % seq)
    alpha = (2.0 * np.pi / seq) * (((np.arange(nr) * tr)[:, None] * col) % seq)
    cb, sb = jnp.asarray(np.cos(beta), F32), jnp.asarray(np.sin(beta), F32)
    ca = jnp.asarray(np.cos(alpha), F32).reshape(nr, 1, seq)
    sa = jnp.asarray(np.sin(alpha), F32).reshape(nr, 1, seq)
    scale = 1.0 / math.sqrt(seq * FOURIER_GROUP_DIM)
    return pl.pallas_call(
        functools.partial(_fourier_long_kernel, scale=scale),
        out_shape=jax.ShapeDtypeStruct((n, fw), BF16),
        grid=(nr, nb),
        in_specs=[_const_spec((tr, seq)), _const_spec((tr, seq)),
                  pl.BlockSpec((1, 1, seq), lambda r, b: (r, 0, 0)),
                  pl.BlockSpec((1, 1, seq), lambda r, b: (r, 0, 0)),
                  pl.BlockSpec((seq, fw2), lambda r, b: (b, 0))],
        out_specs=pl.BlockSpec((tr, fw), lambda r, b: (b * nr + r, 0)),
        scratch_shapes=[pltpu.VMEM((tr, seq), BF16), pltpu.VMEM((tr, seq), BF16)],
        compiler_params=_cparams(("arbitrary", "arbitrary")),
        name="fourier_lat",
    )(cb, sb, ca, sa, acas)


def _attn_kernel(*refs, n_chunks, tk, has_ctx, lam_init):
    lam_ref, sub_ref, q_ref, k_ref, v_ref = refs[:5]
    pos = 5
    if has_ctx:
        kc_ref, vc_ref = refs[5:7]
        pos = 7
    o_ref = refs[pos]
    q = q_ref[...]
    tq, hd2 = q.shape
    half = hd2 // 2
    lane = lax.broadcasted_iota(jnp.int32, q.shape, 1)
    zero = jnp.zeros_like(q)
    q1 = jnp.where(lane < half, q, zero)
    q2 = jnp.where(lane >= half, q, zero)
    dn = (((1,), (1,)), ((), ()))

    def update(state, kc, vc):
        new = []
        for qc, (m, l, acc) in zip((q1, q2), state):
            s = lax.dot_general(qc, kc, dn, preferred_element_type=F32)
            m_new = jnp.maximum(m, jnp.max(s, axis=-1, keepdims=True))
            alpha = jnp.exp2(m - m_new)
            p = jnp.exp2(s - m_new)
            l_new = alpha * l + jnp.sum(p, axis=-1, keepdims=True)
            acc_new = alpha * acc + _dot(p.astype(BF16), vc)
            new.append((m_new, l_new, acc_new))
        return tuple(new)

    init = tuple((jnp.full((tq, 1), -1e30, F32), jnp.zeros((tq, 1), F32), jnp.zeros((tq, hd2), F32))
                 for _ in range(2))

    def body(c, state):
        start = pl.multiple_of(c * tk, tk)
        return update(state, k_ref[pl.ds(start, tk), :], v_ref[pl.ds(start, tk), :])

    state = lax.fori_loop(0, n_chunks, body, init)
    if has_ctx:
        state = update(state, kc_ref[...], vc_ref[...])
    (_, l1, a1), (_, l2, a2) = state
    lp = lam_ref[...]
    lam = (jnp.exp(jnp.sum(lp[0:1] * lp[1:2], axis=-1, keepdims=True))
           - jnp.exp(jnp.sum(lp[2:3] * lp[3:4], axis=-1, keepdims=True)) + lam_init)
    o = a1 / l1 - lam * (a2 / l2)
    o = o * lax.rsqrt(jnp.mean(o * o, axis=-1, keepdims=True) + SUBLN_EPS)
    o_ref[...] = (o * sub_ref[...] * (1.0 - lam_init)).astype(BF16)


def _attention(lam_p, sub_g, q, k, v, k_ctx, v_ctx, *, seq, lam_init):
    n, dw = q.shape
    nb = n // seq
    hd2 = dw // N_HEADS
    tq = min(ATTN_Q_TILE, seq)
    tk = min(ATTN_K_TILE, seq)
    nq = seq // tq
    has_ctx = k_ctx is not None
    in_specs = [_const_spec(lam_p.shape), _const_spec(sub_g.shape),
                pl.BlockSpec((tq, hd2), lambda b, h, i: (b * nq + i, h)),
                pl.BlockSpec((seq, hd2), lambda b, h, i: (b, h)),
                pl.BlockSpec((seq, hd2), lambda b, h, i: (b, h))]
    args = [lam_p, sub_g, q, k, v]
    if has_ctx:
        lc = k_ctx.shape[0] // nb
        in_specs += [pl.BlockSpec((lc, hd2), lambda b, h, i: (b, h))] * 2
        args += [k_ctx, v_ctx]
    return pl.pallas_call(
        functools.partial(_attn_kernel, n_chunks=seq // tk, tk=tk, has_ctx=has_ctx, lam_init=lam_init),
        out_shape=jax.ShapeDtypeStruct((n, dw), BF16),
        grid=(nb, N_HEADS, nq),
        in_specs=in_specs,
        out_specs=pl.BlockSpec((tq, hd2), lambda b, h, i: (b * nq + i, h)),
        compiler_params=_cparams(("parallel", "parallel", "parallel")),
        name="attn_lat" if has_ctx else "attn_ctx",
    )(*args)


def _outproj_kernel(xc_ref, fc_ref, ac_ref, xl_ref, fl_ref, al_ref, w_ref, mod_ref, o_ref, *, ctx_tiles):
    def residual(x_ref, four_ref, attn_ref):
        fw = four_ref.shape[1]
        y = _dot(four_ref[...], w_ref[0:fw, :]) + _dot(attn_ref[...], w_ref[fw:, :])
        o_ref[...] = x_ref[...] + mod_ref[0][2:3] * y

    @pl.when(pl.program_id(0) < ctx_tiles)
    def _():
        residual(xc_ref, fc_ref, ac_ref)

    @pl.when(pl.program_id(0) >= ctx_tiles)
    def _():
        residual(xl_ref, fl_ref, al_ref)


def _outproj(ctx, lat, w_out, mod, *, seq):
    n_ctx, d = ctx[0].shape
    n_lat = lat[0].shape[0]
    tm = TOKEN_TILE
    tc = n_ctx // tm
    ctx_row = lambda i: (jnp.minimum(i, tc - 1), 0)
    lat_row = lambda i: (jnp.maximum(i - tc, 0), 0)
    in_specs = ([pl.BlockSpec((tm, a.shape[1]), ctx_row) for a in ctx]
                + [pl.BlockSpec((tm, a.shape[1]), lat_row) for a in lat]
                + [_const_spec(w_out.shape), pl.BlockSpec((1, 6, d), _seg_index_map(tm, n_ctx, seq))])
    return pl.pallas_call(
        functools.partial(_outproj_kernel, ctx_tiles=tc),
        out_shape=jax.ShapeDtypeStruct((n_ctx + n_lat, d), F32),
        grid=((n_ctx + n_lat) // tm,), in_specs=in_specs,
        out_specs=pl.BlockSpec((tm, d), lambda i: (i, 0)),
        compiler_params=_cparams(("parallel",)),
        name="outproj",
    )(*ctx, *lat, w_out, mod)


def _ffn_kernel(x_ref, mod_ref, g_ref, w1_ref, w3_ref, w2_ref, o_ref):
    m = mod_ref[0]
    x = x_ref[...]
    h = _norm_mod(x, g_ref[...], m[3:4], m[4:5]).astype(BF16)
    a = (_silu(_dot(h, w1_ref[...])) * _dot(h, w3_ref[...])).astype(BF16)
    o_ref[...] = x + m[5:6] * _dot(a, w2_ref[...])


def _ffn(x_all, mod, g, w1, w3, w2, *, n_ctx, seq):
    n, d = x_all.shape
    tm = TOKEN_TILE
    row = lambda i: (i, 0)
    return pl.pallas_call(
        _ffn_kernel,
        out_shape=jax.ShapeDtypeStruct((n, d), F32),
        grid=(n // tm,),
        in_specs=[pl.BlockSpec((tm, d), row), pl.BlockSpec((1, 6, d), _seg_index_map(tm, n_ctx, seq)),
                  _const_spec((1, d)), _const_spec(w1.shape), _const_spec(w3.shape), _const_spec(w2.shape)],
        out_specs=pl.BlockSpec((tm, d), row),
        input_output_aliases={0: 0},
        compiler_params=_cparams(("parallel",)),
        name="ffn_dense",
    )(x_all, mod, g, w1, w3, w2)


def _pw1_kernel(x_ref, mod_ref, g_ref, w_ref, u_ref):
    m = mod_ref[0]
    h = _norm_mod(x_ref[...], g_ref[...], m[0:1], m[1:2]).astype(BF16)
    d = u_ref.shape[1]
    u_ref[...] = _dot(h, w_ref[:, 0:d]) * jax.nn.sigmoid(_dot(h, w_ref[:, d:2 * d]))


def _pw1(x_all, mod, g, w_pw1, *, n_ctx, seq):
    n, d = x_all.shape
    tm = TOKEN_TILE
    row = lambda i: (i, 0)
    return pl.pallas_call(
        _pw1_kernel,
        out_shape=jax.ShapeDtypeStruct((n, d), F32),
        grid=(n // tm,),
        in_specs=[pl.BlockSpec((tm, d), row), pl.BlockSpec((1, 6, d), _seg_index_map(tm, n_ctx, seq)),
                  _const_spec((1, d)), _const_spec(w_pw1.shape)],
        out_specs=pl.BlockSpec((tm, d), row),
        compiler_params=_cparams(("parallel",)),
        name="conv_pw1",
    )(x_all, mod, g, w_pw1)


def _conv_kernel(u_ref, prev_ref, next_ref, wdw_ref, bdw_ref, lng_ref, lnb_ref, w2_ref, x_ref, mod_ref,
                 o_ref, win_scr, conv_scr, *, n_ctx, seq_ctx, seq_lat, taps):
    tm, d = u_ref.shape
    start = pl.program_id(0) * tm
    is_ctx = start < n_ctx
    off = jnp.where(is_ctx, start % seq_ctx, (start - n_ctx) % seq_lat)
    slen = jnp.where(is_ctx, seq_ctx, seq_lat)
    win_scr[0:CONV_HALO, :] = jnp.where(off == 0, 0.0, prev_ref[...])
    win_scr[CONV_HALO:CONV_HALO + tm, :] = u_ref[...]
    win_scr[CONV_HALO + tm:, :] = jnp.where(off + tm == slen, 0.0, next_ref[...])
    base = CONV_HALO - taps // 2
    rc = 64
    for cb in range(d // LANES):
        cols = slice(cb * LANES, (cb + 1) * LANES)
        wcol = wdw_ref[:, cols]
        bias = bdw_ref[:, cols]
        for r0 in range(0, tm, rc):
            acc = jnp.zeros((rc, LANES), F32)
            for j in range(taps):
                acc = acc + win_scr[r0 + base + j:r0 + base + j + rc, cols] * wcol[j:j + 1, :]
            conv_scr[r0:r0 + rc, cols] = acc + bias
    c = conv_scr[...]
    mu = jnp.mean(c, axis=-1, keepdims=True)
    cc = c - mu
    var = jnp.mean(cc * cc, axis=-1, keepdims=True)
    y = _silu(cc * lax.rsqrt(var + NORM_EPS) * lng_ref[...] + lnb_ref[...]).astype(BF16)
    o_ref[...] = x_ref[...] + mod_ref[0][2:3] * _dot(y, w2_ref[...])


def _conv(u, x_all, mod, w_dw, b_dw, ln_g, ln_b, w_pw2, *, n_ctx, seq_ctx, seq_lat):
    n, d = u.shape
    tm = min(CONV_TILE, seq_ctx)
    taps = w_dw.shape[0]
    w_dw = jnp.pad(w_dw, ((0, (-taps) % SUBLANES), (0, 0)))
    hb = tm // CONV_HALO
    last = n // CONV_HALO - 1
    row = lambda i: (i, 0)
    return pl.pallas_call(
        functools.partial(_conv_kernel, n_ctx=n_ctx, seq_ctx=seq_ctx, seq_lat=seq_lat, taps=taps),
        out_shape=jax.ShapeDtypeStruct((n, d), F32),
        grid=(n // tm,),
        in_specs=[pl.BlockSpec((tm, d), row),
                  pl.BlockSpec((CONV_HALO, d), lambda i: (jnp.maximum(i * hb - 1, 0), 0)),
                  pl.BlockSpec((CONV_HALO, d), lambda i: (jnp.minimum((i + 1) * hb, last), 0)),
                  _const_spec(w_dw.shape), _const_spec((1, d)), _const_spec((1, d)), _const_spec((1, d)),
                  _const_spec(w_pw2.shape),
                  pl.BlockSpec((tm, d), row),
                  pl.BlockSpec((1, 6, d), _seg_index_map(tm, n_ctx, seq_lat))],
        out_specs=pl.BlockSpec((tm, d), row),
        scratch_shapes=[pltpu.VMEM((tm + 2 * CONV_HALO, d), F32), pltpu.VMEM((tm, d), F32)],
        input_output_aliases={8: 0},
        compiler_params=_cparams(("parallel",)),
        name="conv_dw_pw2",
    )(u, u, u, w_dw, b_dw, ln_g, ln_b, w_pw2, x_all, mod)


def _router_kernel(x_ref, mod_ref, g_ref, wr_ref, tri_ref, h_ref, info_ref, cnt_ref, carry_scr, *, n_experts):
    @pl.when(pl.program_id(0) == 0)
    def _():
        carry_scr[...] = jnp.zeros_like(carry_scr)
    m = mod_ref[0]
    h = _norm_mod(x_ref[...], g_ref[...], m[3:4], m[4:5])
    h_ref[...] = h
    h_hi = h.astype(BF16)
    h_lo = (h - h_hi.astype(F32)).astype(BF16)
    wr = wr_ref[...]
    w_hi = wr.astype(BF16)
    w_lo = (wr - w_hi.astype(F32)).astype(BF16)
    logits = _dot(h_hi, w_hi) + (_dot(h_lo, w_hi) + _dot(h_hi, w_lo))
    lane = lax.broadcasted_iota(jnp.int32, logits.shape, 1)
    neg = jnp.float32(-jnp.inf)
    logits = jnp.where(lane < n_experts, logits, neg)
    m1 = jnp.max(logits, axis=-1, keepdims=True)
    i1 = jnp.min(jnp.where(logits == m1, lane, LANES), axis=-1, keepdims=True)
    oh1 = lane == i1
    rest = jnp.where(oh1, neg, logits)
    m2 = jnp.max(rest, axis=-1, keepdims=True)
    i2 = jnp.min(jnp.where(rest == m2, lane, LANES), axis=-1, keepdims=True)
    oh2 = lane == i2
    e = jnp.exp(m2 - m1)
    g1 = 1.0 / (1.0 + e)
    g2 = e * g1
    oh = jnp.where(oh1, 1.0, 0.0) + jnp.where(oh2, 1.0, 0.0)
    before = _dot(tri_ref[...], oh.astype(BF16)) + carry_scr[0:1, :]
    r1 = jnp.sum(jnp.where(oh1, before, 0.0), axis=-1, keepdims=True)
    r2 = jnp.sum(jnp.where(oh2, before, 0.0), axis=-1, keepdims=True)
    carry_scr[0:1, :] = carry_scr[0:1, :] + jnp.sum(oh, axis=0, keepdims=True)
    cnt_ref[...] = carry_scr[...]
    info = jnp.where(lane == 0, i1.astype(F32),
           jnp.where(lane == 1, i2.astype(F32),
           jnp.where(lane == 2, r1,
           jnp.where(lane == 3, r2,
           jnp.where(lane == 4, g1,
           jnp.where(lane == 5, g2, 0.0))))))
    info_ref[...] = info


def _router(x_all, mod, g, w_router, *, n_ctx, seq):
    n, d = x_all.shape
    n_experts = w_router.shape[1]
    tm = TOKEN_TILE
    wr = jnp.pad(w_router, ((0, 0), (0, LANES - n_experts)))
    tri = jnp.asarray(np.tril(np.ones((tm, tm), np.float32), -1), BF16)
    row = lambda i: (i, 0)
    return pl.pallas_call(
        functools.partial(_router_kernel, n_experts=n_experts),
        out_shape=[jax.ShapeDtypeStruct((n, d), F32), jax.ShapeDtypeStruct((n, LANES), F32),
                   jax.ShapeDtypeStruct((SUBLANES, LANES), F32)],
        grid=(n // tm,),
        in_specs=[pl.BlockSpec((tm, d), row), pl.BlockSpec((1, 6, d), _seg_index_map(tm, n_ctx, seq)),
                  _const_spec((1, d)), _const_spec(wr.shape), _const_spec(tri.shape)],
        out_specs=[pl.BlockSpec((tm, d), row), pl.BlockSpec((tm, LANES), row),
                   pl.BlockSpec((SUBLANES, LANES), lambda i: (0, 0))],
        scratch_shapes=[pltpu.VMEM((SUBLANES, LANES), F32)],
        compiler_params=_cparams(("arbitrary",)),
        name="moe_router",
    )(x_all, mod, g, wr, tri)


def _dispatch_kernel(zrow_ref, pos_ref, h_ref, xg_ref, zero_scr, sem, zsem, *, n_zero):
    tm = h_ref.shape[0]
    tz = zero_scr.shape[0]

    def zero_copy(e):
        return pltpu.make_async_copy(zero_scr, xg_ref.at[pl.ds(pl.multiple_of(zrow_ref[e], tz), tz)], zsem)

    @pl.when(pl.program_id(0) == 0)
    def _():
        zero_scr[...] = jnp.zeros_like(zero_scr)
        for e in range(n_zero):
            @pl.when(zrow_ref[e] >= 0)
            def _():
                zero_copy(e).start()
        for e in range(n_zero):
            @pl.when(zrow_ref[e] >= 0)
            def _():
                zero_copy(e).wait()

    def issue(r, carry):
        for k in range(TOP_K):
            dst = pos_ref[k * tm + r]
            pltpu.make_async_copy(h_ref.at[pl.ds(r, 1)], xg_ref.at[pl.ds(dst, 1)], sem).start()
        return carry

    lax.fori_loop(0, tm, issue, 0, unroll=8)
    for k in range(TOP_K):
        pltpu.make_async_copy(h_ref, xg_ref.at[pl.ds(0, tm)], sem).wait()


def _dispatch(h_all, pos_tiles, zero_tile_row, n_rows):
    n, d = h_all.shape
    tm = TOKEN_TILE
    n_zero = zero_tile_row.shape[0]
    grid_spec = pltpu.PrefetchScalarGridSpec(
        num_scalar_prefetch=1,
        grid=(n // tm,),
        in_specs=[pl.BlockSpec((TOP_K * tm,), lambda i, last: (i,), memory_space=pltpu.SMEM),
                  pl.BlockSpec((tm, d), lambda i, last: (i, 0))],
        out_specs=pl.BlockSpec(memory_space=pl.ANY),
        scratch_shapes=[pltpu.VMEM((MOE_TILE, d), F32), pltpu.SemaphoreType.DMA(()), pltpu.SemaphoreType.DMA(())],
    )
    return pl.pallas_call(
        functools.partial(_dispatch_kernel, n_zero=n_zero),
        out_shape=jax.ShapeDtypeStruct((n_rows, d), F32),
        grid_spec=grid_spec,
        compiler_params=_cparams(("arbitrary",)),
        name="moe_dispatch",
    )(zero_tile_row, pos_tiles, h_all)


def _experts_kernel(te_ref, nt_ref, x_ref, w1_ref, w3_ref, w2_ref, o_ref, xb_scr):
    c = pl.program_id(1)

    @pl.when(pl.program_id(0) < nt_ref[0])
    def _():
        @pl.when(c == 0)
        def _():
            xb_scr[...] = x_ref[...].astype(BF16)
        xb = xb_scr[...]
        a = (_silu(_dot(xb, w1_ref[0])) * _dot(xb, w3_ref[0])).astype(BF16)
        y = _dot(a, w2_ref[0])

        @pl.when(c == 0)
        def _():
            o_ref[...] = y

        @pl.when(c > 0)
        def _():
            o_ref[...] += y

    @pl.when(jnp.logical_and(pl.program_id(0) >= nt_ref[0], c == 0))
    def _():
        o_ref[...] = jnp.zeros_like(o_ref)


def _experts(xg, w1, w3, w2, tile_expert, n_tiles_used):
    p, d = xg.shape
    tm = MOE_TILE
    ff = w1.shape[2]
    fc = MOE_FF_CHUNK if ff % MOE_FF_CHUNK == 0 else ff
    nc = ff // fc
    n_tiles = p // tm

    def tile(t, nt):
        return jnp.minimum(t, nt[0] - 1)

    def chunk(t, c, nt):
        return jnp.where(t < nt[0], c, nc - 1)

    grid_spec = pltpu.PrefetchScalarGridSpec(
        num_scalar_prefetch=2,
        grid=(n_tiles, nc),
        in_specs=[pl.BlockSpec((tm, d), lambda t, c, te, nt: (tile(t, nt), 0)),
                  pl.BlockSpec((1, d, fc), lambda t, c, te, nt: (te[tile(t, nt)], 0, chunk(t, c, nt))),
                  pl.BlockSpec((1, d, fc), lambda t, c, te, nt: (te[tile(t, nt)], 0, chunk(t, c, nt))),
                  pl.BlockSpec((1, fc, d), lambda t, c, te, nt: (te[tile(t, nt)], chunk(t, c, nt), 0))],
        out_specs=pl.BlockSpec((tm, d), lambda t, c, te, nt: (t, 0)),
        scratch_shapes=[pltpu.VMEM((tm, d), BF16)],
    )
    return pl.pallas_call(
        _experts_kernel,
        out_shape=jax.ShapeDtypeStruct((p, d), F32),
        grid_spec=grid_spec,
        compiler_params=_cparams(("arbitrary", "arbitrary")),
        name="moe_experts",
    )(tile_expert, n_tiles_used, xg, w1, w3, w2)


def _combine_kernel(pos_ref, yg_ref, x_ref, info_ref, mod_ref, fg_ref, o_ref, y_scr, sem):
    tm = x_ref.shape[0]

    def issue(r, carry):
        for k in range(TOP_K):
            src = pos_ref[k * tm + r]
            pltpu.make_async_copy(yg_ref.at[pl.ds(src, 1)], y_scr.at[k, pl.ds(r, 1)], sem).start()
        return carry

    lax.fori_loop(0, tm, issue, 0, unroll=8)
    for k in range(TOP_K):
        pltpu.make_async_copy(yg_ref.at[pl.ds(0, tm)], y_scr.at[k], sem).wait()
    info = info_ref[...]
    y = info[:, 4:5] * y_scr[0] + info[:, 5:6] * y_scr[1]
    x = x_ref[...] + mod_ref[0][5:6] * y
    o_ref[...] = x * lax.rsqrt(jnp.mean(x * x, axis=-1, keepdims=True) + NORM_EPS) * fg_ref[...]


def _combine(yg, x_all, info, pos_tiles, mod, final_g, *, row0, n_rows, seq, mod_row0):
    d = x_all.shape[1]
    tm = TOKEN_TILE
    t0 = row0 // tm
    per_seq = seq // tm
    grid_spec = pltpu.PrefetchScalarGridSpec(
        num_scalar_prefetch=0,
        grid=(n_rows // tm,),
        in_specs=[pl.BlockSpec((TOP_K * tm,), lambda i: (t0 + i,), memory_space=pltpu.SMEM),
                  pl.BlockSpec(memory_space=pl.ANY),
                  pl.BlockSpec((tm, d), lambda i: (t0 + i, 0)),
                  pl.BlockSpec((tm, LANES), lambda i: (t0 + i, 0)),
                  pl.BlockSpec((1, 6, d), lambda i: (mod_row0 + (i // per_seq if mod_row0 else 0), 0, 0)),
                  _const_spec((1, d))],
        out_specs=pl.BlockSpec((tm, d), lambda i: (i, 0)),
        scratch_shapes=[pltpu.VMEM((TOP_K, tm, d), F32), pltpu.SemaphoreType.DMA(())],
    )
    return pl.pallas_call(
        _combine_kernel,
        out_shape=jax.ShapeDtypeStruct((n_rows, d), F32),
        grid_spec=grid_spec,
        compiler_params=_cparams(("arbitrary",)),
        name="moe_combine",
    )(pos_tiles, yg, x_all, info, mod, final_g)


def _moe(x_all, mod, g, w_router, w1, w3, w2, final_g, *, n_ctx, seq_ctx_total, seq_lat):
    n, d = x_all.shape
    n_experts = w_router.shape[1]
    tm = TOKEN_TILE
    h_all, info, counts = _router(x_all, mod, g, w_router, n_ctx=n_ctx, seq=seq_lat)
    cnt = counts[0, :n_experts].astype(jnp.int32)
    padded = ((cnt + MOE_TILE - 1) // MOE_TILE) * MOE_TILE
    ends = jnp.cumsum(padded)
    offs = ends - padded
    n_tiles = (TOP_K * n) // MOE_TILE + n_experts
    n_used = (ends[-1] // MOE_TILE).astype(jnp.int32)
    tile_start = jnp.arange(n_tiles, dtype=jnp.int32) * MOE_TILE
    tile_expert = jnp.minimum(jnp.sum(tile_start[:, None] >= ends[None, :], axis=1), n_experts - 1).astype(jnp.int32)
    tile_expert = jnp.where(tile_start < ends[-1], tile_expert, tile_expert[jnp.maximum(n_used - 1, 0)])
    tail_start = tile_start[n_tiles - n_experts:]
    zero_tile_row = jnp.concatenate([jnp.where(padded > 0, ends - MOE_TILE, -1),
                                     jnp.where(tail_start >= ends[-1], tail_start, -1)]).astype(jnp.int32)
    e_idx = info[:, 0:TOP_K].astype(jnp.int32)
    pos = offs[e_idx] + info[:, TOP_K:2 * TOP_K].astype(jnp.int32)
    pos_tiles = pos.reshape(n // tm, tm, TOP_K).transpose(0, 2, 1).reshape(-1)
    xg = _dispatch(h_all, pos_tiles, zero_tile_row, n_tiles * MOE_TILE)
    yg = _experts(xg, w1, w3, w2, tile_expert, n_used.reshape(1))
    y_ctx = _combine(yg, x_all, info, pos_tiles, mod, final_g, row0=0, n_rows=n_ctx, seq=seq_ctx_total, mod_row0=0)
    y_lat = _combine(yg, x_all, info, pos_tiles, mod, final_g, row0=n_ctx, n_rows=n - n_ctx, seq=seq_lat, mod_row0=1)
    return y_ctx, y_lat


def _rope_tables(seq, hd):
    axis_dim = hd // 2
    nf = axis_dim // 2
    inv = jnp.power(ROPE_BASE, -jnp.arange(nf, dtype=F32) / nf)
    pos = np.arange(seq)
    row = jnp.asarray(pos // GRID_W, F32)
    col = jnp.asarray(pos % GRID_W, F32)
    lane = np.arange(2 * hd)
    dd = lane % hd
    use_col = (dd // axis_dim) == 1
    e = dd % axis_dim
    first = jnp.asarray(e < nf)[None, :]
    freq = inv[e % nf]
    ang = jnp.where(jnp.asarray(use_col)[None, :], col[:, None], row[:, None]) * freq[None, :]
    cos, sin = jnp.cos(ang), jnp.sin(ang)
    return cos, jnp.where(first, -sin, 0.0), jnp.where(first, 0.0, sin)


def _channel_dft(fw):
    g = FOURIER_GROUP_DIM
    ang = (2.0 * np.pi / g) * ((np.arange(g)[:, None] * np.arange(g)[None, :]) % g)
    eye = np.eye(fw // g)
    return jnp.asarray(np.concatenate([np.kron(eye, np.cos(ang)), np.kron(eye, np.sin(ang))], axis=1), BF16)


def kernel(x_prompt, x_sample, cache_k, cache_v, c, c_ctx, w_ada, b_ada, norm_g, final_g, w_in_ab, w_out_ab,
           diff_lambda, diff_subln, conv_w_pw1, conv_w_dw, conv_b_dw, conv_ln_g, conv_ln_b, conv_w_pw2,
           ffn_w1, ffn_w3, ffn_w2, moe_router, moe_w1, moe_w3, moe_w2):
    bp, lp, d = x_prompt.shape
    bs, ls, _ = x_sample.shape
    lc = cache_k.shape[2]
    n_ctx, n_lat = bp * lp, bs * ls
    n_all = n_ctx + n_lat
    dw = N_HEADS * cache_k.shape[-1]
    hd = cache_k.shape[-1] // 2
    fw = w_in_ab.shape[2] - 3 * dw

    rows = -(-(1 + bs) // SUBLANES) * SUBLANES
    cond = jnp.zeros((rows, d), F32).at[0].set(c_ctx).at[1:1 + bs].set(c)
    mod = _modulation(cond, w_ada, b_ada)

    lam_init = 0.8 - 0.6 * math.exp(-0.3 * 0)
    q_scale = (hd ** -0.5) * LOG2E
    w_in = w_in_ab[0].astype(BF16)
    bd = _channel_dft(fw)
    g00 = norm_g[0, 0].reshape(1, d)
    xp = x_prompt.reshape(n_ctx, d)
    xs = x_sample.reshape(n_lat, d)
    acas_p, q_p, k_p, v_p, kc_new, vc_new = _inproj(
        xp, mod[0], g00, w_in, bd, seq=lp, mod_row0=0, rope_tables=None, cache=True, q_scale=q_scale)
    acas_s, q_s, k_s, v_s = _inproj(
        xs, mod[0], g00, w_in, bd, seq=ls, mod_row0=1, rope_tables=_rope_tables(ls, hd), cache=False, q_scale=q_scale)
    four_p = _fourier_short(acas_p, lp)
    four_s = _fourier_long(acas_s, ls)
    lam_p = diff_lambda[0]
    sub_g = diff_subln[0].reshape(1, 2 * hd)
    attn_p = _attention(lam_p, sub_g, q_p, k_p, v_p, None, None, seq=lp, lam_init=lam_init)
    k_ctx = cache_k[:, 0].reshape(bs * lc, dw).astype(BF16)
    v_ctx = cache_v[:, 0].reshape(bs * lc, dw).astype(BF16)
    attn_s = _attention(lam_p, sub_g, q_s, k_s, v_s, k_ctx, v_ctx, seq=ls, lam_init=lam_init)
    w_out = w_out_ab[0].astype(BF16)
    x_all = _outproj((xp, four_p, attn_p), (xs, four_s, attn_s), w_out, mod[0], seq=ls)
    x_all = _ffn(x_all, mod[0], norm_g[0, 1].reshape(1, d), ffn_w1[0].astype(BF16), ffn_w3[0].astype(BF16),
                 ffn_w2[0].astype(BF16), n_ctx=n_ctx, seq=ls)

    u = _pw1(x_all, mod[1], norm_g[1, 0].reshape(1, d), conv_w_pw1[0].astype(BF16), n_ctx=n_ctx, seq=ls)
    x_all = _conv(u, x_all, mod[1], conv_w_dw[0], conv_b_dw[0].reshape(1, d), conv_ln_g[0].reshape(1, d),
                  conv_ln_b[0].reshape(1, d), conv_w_pw2[0].astype(BF16), n_ctx=n_ctx, seq_ctx=lp, seq_lat=ls)
    y_ctx, y_lat = _moe(x_all, mod[1], norm_g[1, 1].reshape(1, d), moe_router[0], moe_w1[0].astype(BF16),
                        moe_w3[0].astype(BF16), moe_w2[0].astype(BF16), final_g.reshape(1, d),
                        n_ctx=n_ctx, seq_ctx_total=n_ctx, seq_lat=ls)

    n_even = w_in_ab.shape[0]
    new_k = kc_new.reshape(bp, n_even, lp, N_HEADS, 2 * hd)
    new_v = vc_new.reshape(bp, n_even, lp, N_HEADS, 2 * hd)
    return (y_ctx.reshape(bp, lp, d), y_lat.reshape(bs, ls, d), new_k, new_v)
```

```python
import functools
import math

import numpy as np
import jax
import jax.numpy as jnp
from jax import lax
from jax.experimental import pallas as pl
from jax.experimental.pallas import tpu as pltpu

F32 = jnp.float32
BF16 = jnp.bfloat16

GRID_W = 64
ROPE_BASE = 10000.0
N_HEADS = 4
FOURIER_GROUP_DIM = 128
CONV_HALO = 16
TOP_K = 2
NORM_EPS = 1e-6
SUBLN_EPS = 1e-5
LANES = 128
SUBLANES = 8
LOG2E = 1.4426950408889634

TOKEN_TILE = 512
CONV_TILE = 256
CONV_ROW_CHUNK = 64
ATTN_Q_TILE = 256
ATTN_K_TILE = 512
FOURIER_ROW_TILE = 512
FOURIER_TABLE_ROWS = 256
MOE_TILE = 512
MOE_FF_CHUNK = 1792
VMEM_LIMIT = 56 * 1024 * 1024


def _cparams(semantics, vmem=VMEM_LIMIT):
    return pltpu.CompilerParams(dimension_semantics=semantics, vmem_limit_bytes=vmem)


def _const_spec(shape):
    nd = len(shape)
    return pl.BlockSpec(shape, lambda *_: (0,) * nd, pipeline_mode=pl.Buffered(1))


def _dot(a, b):
    return jnp.dot(a, b, preferred_element_type=F32)


def _silu(x):
    return x * jax.nn.sigmoid(x)


def _norm_mod(x, g, shift, scale):
    ms = jnp.mean(x * x, axis=-1, keepdims=True)
    return (x * lax.rsqrt(ms + NORM_EPS)) * (g * (1.0 + scale)) + shift


def _seg_index_map(tile, n_ctx, seq):
    def index_map(i, *_):
        start = i * tile
        return (jnp.where(start < n_ctx, 0, 1 + (start - n_ctx) // seq), 0, 0)
    return index_map


def _mod_kernel(c_ref, w_ref, b_ref, o_ref):
    c = c_ref[...]
    o_ref[0] = _dot(_silu(c).astype(BF16), w_ref[0].astype(BF16)) + b_ref[0]


def _modulation(cond, w_ada, b_ada):
    depth, d, n6 = w_ada.shape
    rows = cond.shape[0]
    tn = 1536
    out = pl.pallas_call(
        _mod_kernel,
        out_shape=jax.ShapeDtypeStruct((depth, rows, n6), F32),
        grid=(depth, n6 // tn),
        in_specs=[pl.BlockSpec((rows, d), lambda l, j: (0, 0)),
                  pl.BlockSpec((1, d, tn), lambda l, j: (l, 0, j)),
                  pl.BlockSpec((1, 1, tn), lambda l, j: (l, 0, j))],
        out_specs=pl.BlockSpec((1, rows, tn), lambda l, j: (l, 0, j)),
        compiler_params=_cparams(("parallel", "parallel")),
        name="modulation",
    )(cond, w_ada, b_ada.reshape(depth, 1, n6))
    return out.reshape(depth, rows, 6, d)


def _rope(t, cos, sin_lo, sin_hi):
    outs = []
    for h in range(t.shape[1] // LANES):
        th = t[:, h * LANES:(h + 1) * LANES]
        outs.append(th * cos + pltpu.roll(th, LANES - 16, 1) * sin_lo + pltpu.roll(th, 16, 1) * sin_hi)
    return jnp.concatenate(outs, axis=-1)


def _inproj_kernel(*refs, rope, cache, q_scale):
    x_ref, mod_ref, g_ref, w_ref, bd_ref = refs[:5]
    pos = 5
    if rope:
        cos_ref, slo_ref, shi_ref = refs[pos:pos + 3]
        pos += 3
    acas_ref, q_ref, k_ref, v_ref = refs[pos:pos + 4]
    m = mod_ref[0]
    h = _norm_mod(x_ref[...], g_ref[...], m[0:1], m[1:2]).astype(BF16)
    fw = bd_ref.shape[0]
    dw = q_ref.shape[1]
    f = _dot(h, w_ref[:, 0:fw])
    acas_ref[...] = _dot(f.astype(BF16), bd_ref[...]).astype(BF16)
    q = _dot(h, w_ref[:, fw:fw + dw])
    k = _dot(h, w_ref[:, fw + dw:fw + 2 * dw])
    v = _dot(h, w_ref[:, fw + 2 * dw:fw + 3 * dw])
    if cache:
        kc_ref, vc_ref = refs[pos + 4:pos + 6]
        kc_ref[...] = k
        vc_ref[...] = v
    if rope:
        cos, slo, shi = cos_ref[...], slo_ref[...], shi_ref[...]
        q = _rope(q, cos, slo, shi)
        k = _rope(k, cos, slo, shi)
    q_ref[...] = (q * q_scale).astype(BF16)
    k_ref[...] = k.astype(BF16)
    v_ref[...] = v.astype(BF16)


def _inproj(x, mod, g, w_in, bd, *, seq, mod_row0, rope_tables, cache, q_scale):
    n, d = x.shape
    tm = min(TOKEN_TILE, seq)
    fw = bd.shape[0]
    dw = (w_in.shape[1] - fw) // 3
    per_seq = seq // tm
    row = lambda i: (i, 0)
    in_specs = [pl.BlockSpec((tm, d), row),
                pl.BlockSpec((1, 6, d), lambda i: (mod_row0 + (i // per_seq if mod_row0 else 0), 0, 0)),
                _const_spec((1, d)), _const_spec(w_in.shape), _const_spec(bd.shape)]
    args = [x, mod, g, w_in, bd]
    if rope_tables is not None:
        in_specs += [pl.BlockSpec((tm, LANES), lambda i: (i % per_seq, 0))] * 3
        args += list(rope_tables)
    out_shape = [jax.ShapeDtypeStruct((n, 2 * fw), BF16)] + [jax.ShapeDtypeStruct((n, dw), BF16)] * 3
    out_specs = [pl.BlockSpec((tm, 2 * fw), row)] + [pl.BlockSpec((tm, dw), row)] * 3
    if cache:
        out_shape += [jax.ShapeDtypeStruct((n, dw), F32)] * 2
        out_specs += [pl.BlockSpec((tm, dw), row)] * 2
    return pl.pallas_call(
        functools.partial(_inproj_kernel, rope=rope_tables is not None, cache=cache, q_scale=q_scale),
        out_shape=out_shape, grid=(n // tm,), in_specs=in_specs, out_specs=out_specs,
        compiler_params=_cparams(("parallel",)),
        name="inproj_ctx" if cache else "inproj_lat",
    )(*args)


def _fourier_short_kernel(ct_ref, st_ref, acas_ref, o_ref, *, scale):
    fw = o_ref.shape[1]
    y = _dot(ct_ref[...], acas_ref[:, 0:fw]) - _dot(st_ref[...], acas_ref[:, fw:2 * fw])
    o_ref[...] = (y * scale).astype(BF16)


def _fourier_short(acas, seq):
    n, fw2 = acas.shape
    fw = fw2 // 2
    idx = (np.arange(seq)[:, None] * np.arange(seq)[None, :]) % seq
    ang = (2.0 * np.pi / seq) * idx
    ct = jnp.asarray(np.cos(ang), BF16)
    st = jnp.asarray(np.sin(ang), BF16)
    scale = 1.0 / math.sqrt(seq * FOURIER_GROUP_DIM)
    return pl.pallas_call(
        functools.partial(_fourier_short_kernel, scale=scale),
        out_shape=jax.ShapeDtypeStruct((n, fw), BF16),
        grid=(n // seq,),
        in_specs=[_const_spec((seq, seq)), _const_spec((seq, seq)),
                  pl.BlockSpec((seq, fw2), lambda b: (b, 0))],
        out_specs=pl.BlockSpec((seq, fw), lambda b: (b, 0)),
        compiler_params=_cparams(("parallel",)),
        name="fourier_ctx",
    )(ct, st, acas)


def _fourier_long_kernel(cb_ref, sb_ref, ca_ref, sa_ref, acas_ref, o_ref, ct_scr, st_scr, *, scale):
    @pl.when(pl.program_id(1) == 0)
    def _():
        tb, seq = cb_ref.shape
        cw = 512
        for part in range(ca_ref.shape[0]):
            rows = slice(part * tb, (part + 1) * tb)
            for c0 in range(0, seq, cw):
                ca, sa = ca_ref[part, :, c0:c0 + cw], sa_ref[part, :, c0:c0 + cw]
                cb, sb = cb_ref[:, c0:c0 + cw], sb_ref[:, c0:c0 + cw]
                ct_scr[rows, c0:c0 + cw] = (ca * cb - sa * sb).astype(BF16)
                st_scr[rows, c0:c0 + cw] = (sa * cb + ca * sb).astype(BF16)
    fw = o_ref.shape[1]
    y = _dot(ct_scr[...], acas_ref[:, 0:fw]) - _dot(st_scr[...], acas_ref[:, fw:2 * fw])
    o_ref[...] = (y * scale).astype(BF16)


def _fourier_long(acas, seq):
    n, fw2 = acas.shape
    fw = fw2 // 2
    nb = n // seq
    tr = min(FOURIER_ROW_TILE, seq)
    tb = min(FOURIER_TABLE_ROWS, tr)
    parts = tr // tb
    nr = seq // tr
    col = np.arange(seq)[None, :]
    beta = (2.0 * np.pi / seq) * ((np.arange(tb)[:, None] * col) % seq)
    alpha = (2.0 * np.pi / seq) * (((np.arange(seq // tb) * tb)[:, None] * col) % seq)
    cb, sb = jnp.asarray(np.cos(beta), F32), jnp.asarray(np.sin(beta), F32)
    ca = jnp.asarray(np.cos(alpha), F32).reshape(seq // tb, 1, seq)
    sa = jnp.asarray(np.sin(alpha), F32).reshape(seq // tb, 1, seq)
    scale = 1.0 / math.sqrt(seq * FOURIER_GROUP_DIM)
    return pl.pallas_call(
        functools.partial(_fourier_long_kernel, scale=scale),
        out_shape=jax.ShapeDtypeStruct((n, fw), BF16),
        grid=(nr, nb),
        in_specs=[_const_spec((tb, seq)), _const_spec((tb, seq)),
                  pl.BlockSpec((parts, 1, seq), lambda r, b: (r, 0, 0)),
                  pl.BlockSpec((parts, 1, seq), lambda r, b: (r, 0, 0)),
                  pl.BlockSpec((seq, fw2), lambda r, b: (b, 0))],
        out_specs=pl.BlockSpec((tr, fw), lambda r, b: (b * nr + r, 0)),
        scratch_shapes=[pltpu.VMEM((tr, seq), BF16), pltpu.VMEM((tr, seq), BF16)],
        compiler_params=_cparams(("arbitrary", "arbitrary")),
        name="fourier_lat",
    )(cb, sb, ca, sa, acas)


def _col_reduce(x, op):
    r = x.reshape(x.shape[0] // SUBLANES, SUBLANES, x.shape[1])
    n = r.shape[0]
    while n > 1:
        n //= 2
        r = op(r[:n], r[n:2 * n])
    r = r[0]
    for shift in (4, 2, 1):
        r = op(r, pltpu.roll(r, shift, 0))
    return r[0:1]


def _attn_kernel(*refs, n_chunks, tk, has_ctx, lam_init):
    lam_ref, sub_ref, q_ref, k_ref, v_ref = refs[:5]
    pos = 5
    if has_ctx:
        kc_ref, vc_ref = refs[5:7]
        pos = 7
    o_ref, vt_scr, acc_scr, s_scr = refs[pos:pos + 4]
    if has_ctx:
        vtc_scr = refs[pos + 4]
    tq, hd2 = q_ref.shape
    half = hd2 // 2

    @pl.when(pl.program_id(2) == 0)
    def _():
        for c in range(n_chunks):
            vt_scr[c] = v_ref[c * tk:(c + 1) * tk, :].astype(F32).T.astype(BF16)
        if has_ctx:
            vtc_scr[...] = vc_ref[...].astype(F32).T.astype(BF16)

    qt = q_ref[...].astype(F32).T
    row = lax.broadcasted_iota(jnp.int32, qt.shape, 0)
    qts = (jnp.where(row < half, qt, 0.0).astype(BF16),
           jnp.where(row >= half, qt, 0.0).astype(BF16))

    chunks = [(k_ref, c * tk, tk, functools.partial(vt_scr.__getitem__, c)) for c in range(n_chunks)]
    if has_ctx:
        chunks.append((kc_ref, 0, kc_ref.shape[0], functools.partial(vtc_scr.__getitem__, Ellipsis)))

    def scores(idx):
        k_src, start, size, _ = chunks[idx]
        kc = k_src[start:start + size, :]
        for j in range(2):
            s_scr[idx % 2, j, 0:size, :] = _dot(kc, qts[j])

    def softmax_pv(idx, state):
        _, _, size, vt = chunks[idx]
        new = []
        for j in range(2):
            s = s_scr[idx % 2, j, 0:size, :]
            m_blk = _col_reduce(s, jnp.maximum)
            if state is None:
                m_new = m_blk
                p = jnp.exp2(s - m_new)
                l_new = _col_reduce(p, jnp.add)
                acc_scr[j] = _dot(vt(), p.astype(BF16))
            else:
                m, l = state[j]
                m_new = jnp.maximum(m, m_blk)
                alpha = jnp.exp2(m - m_new)
                p = jnp.exp2(s - m_new)
                l_new = alpha * l + _col_reduce(p, jnp.add)
                acc_scr[j] = alpha * acc_scr[j] + _dot(vt(), p.astype(BF16))
            new.append((m_new, l_new))
        return tuple(new)

    state = None
    scores(0)
    for idx in range(len(chunks)):
        if idx + 1 < len(chunks):
            scores(idx + 1)
        state = softmax_pv(idx, state)
    (_, l1), (_, l2) = state
    lp = lam_ref[...]
    lam = (jnp.exp(jnp.sum(lp[0:1] * lp[1:2], axis=-1, keepdims=True))
           - jnp.exp(jnp.sum(lp[2:3] * lp[3:4], axis=-1, keepdims=True)) + lam_init)
    ot = acc_scr[0] * (1.0 / l1) - lam * (acc_scr[1] * (1.0 / l2))
    o = ot.T
    o = o * lax.rsqrt(jnp.mean(o * o, axis=-1, keepdims=True) + SUBLN_EPS)
    o_ref[...] = (o * sub_ref[...] * (1.0 - lam_init)).astype(BF16)


def _attention(lam_p, sub_g, q, k, v, k_ctx, v_ctx, *, seq, lam_init):
    n, dw = q.shape
    nb = n // seq
    hd2 = dw // N_HEADS
    tq = min(ATTN_Q_TILE, seq)
    tk = min(ATTN_K_TILE, seq)
    nq = seq // tq
    has_ctx = k_ctx is not None
    in_specs = [_const_spec(lam_p.shape), _const_spec(sub_g.shape),
                pl.BlockSpec((tq, hd2), lambda b, h, i: (b * nq + i, h)),
                pl.BlockSpec((seq, hd2), lambda b, h, i: (b, h)),
                pl.BlockSpec((seq, hd2), lambda b, h, i: (b, h))]
    args = [lam_p, sub_g, q, k, v]
    scratch = [pltpu.VMEM((seq // tk, hd2, tk), BF16), pltpu.VMEM((2, hd2, tq), F32),
               pltpu.VMEM((2, 2, tk, tq), F32)]
    if has_ctx:
        lc = k_ctx.shape[0] // nb
        in_specs += [pl.BlockSpec((lc, hd2), lambda b, h, i: (b, h))] * 2
        args += [k_ctx, v_ctx]
        scratch.append(pltpu.VMEM((hd2, lc), BF16))
    return pl.pallas_call(
        functools.partial(_attn_kernel, n_chunks=seq // tk, tk=tk, has_ctx=has_ctx, lam_init=lam_init),
        out_shape=jax.ShapeDtypeStruct((n, dw), BF16),
        grid=(nb, N_HEADS, nq),
        in_specs=in_specs,
        out_specs=pl.BlockSpec((tq, hd2), lambda b, h, i: (b * nq + i, h)),
        scratch_shapes=scratch,
        compiler_params=_cparams(("parallel", "parallel", "arbitrary")),
        name="attn_lat" if has_ctx else "attn_ctx",
    )(*args)


def _outproj_kernel(xc_ref, fc_ref, ac_ref, xl_ref, fl_ref, al_ref, w_ref, mod_ref, o_ref, *, ctx_tiles):
    def residual(x_ref, four_ref, attn_ref):
        fw = four_ref.shape[1]
        y = _dot(four_ref[...], w_ref[0:fw, :]) + _dot(attn_ref[...], w_ref[fw:, :])
        o_ref[...] = x_ref[...] + mod_ref[0][2:3] * y

    @pl.when(pl.program_id(0) < ctx_tiles)
    def _():
        residual(xc_ref, fc_ref, ac_ref)

    @pl.when(pl.program_id(0) >= ctx_tiles)
    def _():
        residual(xl_ref, fl_ref, al_ref)


def _outproj(ctx, lat, w_out, mod, *, seq):
    n_ctx, d = ctx[0].shape
    n_lat = lat[0].shape[0]
    tm = TOKEN_TILE
    tc = n_ctx // tm
    ctx_row = lambda i: (jnp.minimum(i, tc - 1), 0)
    lat_row = lambda i: (jnp.maximum(i - tc, 0), 0)
    in_specs = ([pl.BlockSpec((tm, a.shape[1]), ctx_row) for a in ctx]
                + [pl.BlockSpec((tm, a.shape[1]), lat_row) for a in lat]
                + [_const_spec(w_out.shape), pl.BlockSpec((1, 6, d), _seg_index_map(tm, n_ctx, seq))])
    return pl.pallas_call(
        functools.partial(_outproj_kernel, ctx_tiles=tc),
        out_shape=jax.ShapeDtypeStruct((n_ctx + n_lat, d), F32),
        grid=((n_ctx + n_lat) // tm,), in_specs=in_specs,
        out_specs=pl.BlockSpec((tm, d), lambda i: (i, 0)),
        compiler_params=_cparams(("parallel",)),
        name="outproj",
    )(*ctx, *lat, w_out, mod)


def _ffn_kernel(x_ref, mod_ref, g_ref, w1_ref, w3_ref, w2_ref, o_ref):
    m = mod_ref[0]
    x = x_ref[...]
    h = _norm_mod(x, g_ref[...], m[3:4], m[4:5]).astype(BF16)
    a = (_silu(_dot(h, w1_ref[...])) * _dot(h, w3_ref[...])).astype(BF16)
    o_ref[...] = x + m[5:6] * _dot(a, w2_ref[...])


def _ffn(x_all, mod, g, w1, w3, w2, *, n_ctx, seq):
    n, d = x_all.shape
    tm = TOKEN_TILE
    row = lambda i: (i, 0)
    return pl.pallas_call(
        _ffn_kernel,
        out_shape=jax.ShapeDtypeStruct((n, d), F32),
        grid=(n // tm,),
        in_specs=[pl.BlockSpec((tm, d), row), pl.BlockSpec((1, 6, d), _seg_index_map(tm, n_ctx, seq)),
                  _const_spec((1, d)), _const_spec(w1.shape), _const_spec(w3.shape), _const_spec(w2.shape)],
        out_specs=pl.BlockSpec((tm, d), row),
        input_output_aliases={0: 0},
        compiler_params=_cparams(("parallel",)),
        name="ffn_dense",
    )(x_all, mod, g, w1, w3, w2)


def _pw1_kernel(x_ref, mod_ref, g_ref, w_ref, u_ref):
    m = mod_ref[0]
    h = _norm_mod(x_ref[...], g_ref[...], m[0:1], m[1:2]).astype(BF16)
    d = u_ref.shape[1]
    u_ref[...] = _dot(h, w_ref[:, 0:d]) * jax.nn.sigmoid(_dot(h, w_ref[:, d:2 * d]))


def _pw1(x_all, mod, g, w_pw1, *, n_ctx, seq):
    n, d = x_all.shape
    tm = TOKEN_TILE
    row = lambda i: (i, 0)
    return pl.pallas_call(
        _pw1_kernel,
        out_shape=jax.ShapeDtypeStruct((n, d), F32),
        grid=(n // tm,),
        in_specs=[pl.BlockSpec((tm, d), row), pl.BlockSpec((1, 6, d), _seg_index_map(tm, n_ctx, seq)),
                  _const_spec((1, d)), _const_spec(w_pw1.shape)],
        out_specs=pl.BlockSpec((tm, d), row),
        compiler_params=_cparams(("parallel",)),
        name="conv_pw1",
    )(x_all, mod, g, w_pw1)


def _conv_kernel(u_ref, prev_ref, next_ref, wdw_ref, bdw_ref, lng_ref, lnb_ref, w2_ref, x_ref, mod_ref,
                 o_ref, win_scr, conv_scr, shift_scr, *, n_ctx, seq_ctx, seq_lat, taps, front):
    tm, d = u_ref.shape
    start = pl.program_id(0) * tm
    is_ctx = start < n_ctx
    off = jnp.where(is_ctx, start % seq_ctx, (start - n_ctx) % seq_lat)
    slen = jnp.where(is_ctx, seq_ctx, seq_lat)
    win_scr[0:CONV_HALO, :] = jnp.where(off == 0, 0.0, prev_ref[...])
    win_scr[CONV_HALO:CONV_HALO + tm, :] = u_ref[...]
    win_scr[CONV_HALO + tm:, :] = jnp.where(off + tm == slen, 0.0, next_ref[...])
    n_q = wdw_ref.shape[0]
    rc = CONV_ROW_CHUNK
    for cb in range(d // LANES):
        cols = slice(cb * LANES, (cb + 1) * LANES)
        for k in range(1, SUBLANES):
            shift_scr[k] = win_scr[k:k + tm + n_q - SUBLANES, cols]
        w = {q: wdw_ref[q, :, cols] for q in range(front, front + taps)}
        bias = bdw_ref[:, cols]
        for r0 in range(0, tm, rc):
            acc = None
            for q in range(front, front + taps):
                k = q % SUBLANES
                if k:
                    tap = shift_scr[k, r0 + q - k:r0 + q - k + rc, :]
                else:
                    tap = win_scr[r0 + q:r0 + q + rc, cols]
                term = tap.reshape(rc // SUBLANES, SUBLANES, LANES) * w[q][None]
                acc = term if acc is None else acc + term
            conv_scr[r0:r0 + rc, cols] = acc.reshape(rc, LANES) + bias
    c = conv_scr[...]
    mu = jnp.mean(c, axis=-1, keepdims=True)
    cc = c - mu
    var = jnp.mean(cc * cc, axis=-1, keepdims=True)
    y = _silu(cc * lax.rsqrt(var + NORM_EPS) * lng_ref[...] + lnb_ref[...]).astype(BF16)
    o_ref[...] = x_ref[...] + mod_ref[0][2:3] * _dot(y, w2_ref[...])


def _conv(u, x_all, mod, w_dw, b_dw, ln_g, ln_b, w_pw2, *, n_ctx, seq_ctx, seq_lat):
    n, d = u.shape
    tm = min(CONV_TILE, seq_ctx)
    taps = w_dw.shape[0]
    front = CONV_HALO - taps // 2
    w_dw = jnp.pad(w_dw, ((front, 2 * CONV_HALO - taps - front), (0, 0)))
    w_dw = jnp.broadcast_to(w_dw[:, None, :], (2 * CONV_HALO, SUBLANES, d))
    hb = tm // CONV_HALO
    last = n // CONV_HALO - 1
    row = lambda i: (i, 0)
    return pl.pallas_call(
        functools.partial(_conv_kernel, n_ctx=n_ctx, seq_ctx=seq_ctx, seq_lat=seq_lat, taps=taps, front=front),
        out_shape=jax.ShapeDtypeStruct((n, d), F32),
        grid=(n // tm,),
        in_specs=[pl.BlockSpec((tm, d), row),
                  pl.BlockSpec((CONV_HALO, d), lambda i: (jnp.maximum(i * hb - 1, 0), 0)),
                  pl.BlockSpec((CONV_HALO, d), lambda i: (jnp.minimum((i + 1) * hb, last), 0)),
                  _const_spec(w_dw.shape), _const_spec((1, d)), _const_spec((1, d)), _const_spec((1, d)),
                  _const_spec(w_pw2.shape),
                  pl.BlockSpec((tm, d), row),
                  pl.BlockSpec((1, 6, d), _seg_index_map(tm, n_ctx, seq_lat))],
        out_specs=pl.BlockSpec((tm, d), row),
        scratch_shapes=[pltpu.VMEM((tm + 2 * CONV_HALO, d), F32), pltpu.VMEM((tm, d), F32),
                        pltpu.VMEM((SUBLANES, tm + 2 * CONV_HALO - SUBLANES, LANES), F32)],
        input_output_aliases={8: 0},
        compiler_params=_cparams(("parallel",)),
        name="conv_dw_pw2",
    )(u, u, u, w_dw, b_dw, ln_g, ln_b, w_pw2, x_all, mod)


def _router_kernel(x_ref, mod_ref, g_ref, wr_ref, tri_ref, h_ref, info_ref, cnt_ref, carry_scr, *, n_experts):
    @pl.when(pl.program_id(0) == 0)
    def _():
        carry_scr[...] = jnp.zeros_like(carry_scr)
    m = mod_ref[0]
    h = _norm_mod(x_ref[...], g_ref[...], m[3:4], m[4:5])
    h_ref[...] = h
    h_hi = h.astype(BF16)
    h_lo = (h - h_hi.astype(F32)).astype(BF16)
    wr = wr_ref[...]
    w_hi = wr.astype(BF16)
    w_lo = (wr - w_hi.astype(F32)).astype(BF16)
    logits = _dot(h_hi, w_hi) + (_dot(h_lo, w_hi) + _dot(h_hi, w_lo))
    lane = lax.broadcasted_iota(jnp.int32, logits.shape, 1)
    neg = jnp.float32(-jnp.inf)
    logits = jnp.where(lane < n_experts, logits, neg)
    m1 = jnp.max(logits, axis=-1, keepdims=True)
    i1 = jnp.min(jnp.where(logits == m1, lane, LANES), axis=-1, keepdims=True)
    oh1 = lane == i1
    rest = jnp.where(oh1, neg, logits)
    m2 = jnp.max(rest, axis=-1, keepdims=True)
    i2 = jnp.min(jnp.where(rest == m2, lane, LANES), axis=-1, keepdims=True)
    oh2 = lane == i2
    e = jnp.exp(m2 - m1)
    g1 = 1.0 / (1.0 + e)
    g2 = e * g1
    oh = jnp.where(oh1, 1.0, 0.0) + jnp.where(oh2, 1.0, 0.0)
    before = _dot(tri_ref[...], oh.astype(BF16)) + carry_scr[0:1, :]
    r1 = jnp.sum(jnp.where(oh1, before, 0.0), axis=-1, keepdims=True)
    r2 = jnp.sum(jnp.where(oh2, before, 0.0), axis=-1, keepdims=True)
    carry_scr[0:1, :] = carry_scr[0:1, :] + jnp.sum(oh, axis=0, keepdims=True)
    cnt_ref[...] = carry_scr[...]
    info = jnp.where(lane == 0, i1.astype(F32),
           jnp.where(lane == 1, i2.astype(F32),
           jnp.where(lane == 2, r1,
           jnp.where(lane == 3, r2,
           jnp.where(lane == 4, g1,
           jnp.where(lane == 5, g2, 0.0))))))
    info_ref[...] = info


def _router(x_all, mod, g, w_router, *, n_ctx, seq):
    n, d = x_all.shape
    n_experts = w_router.shape[1]
    tm = TOKEN_TILE
    wr = jnp.pad(w_router, ((0, 0), (0, LANES - n_experts)))
    tri = jnp.asarray(np.tril(np.ones((tm, tm), np.float32), -1), BF16)
    row = lambda i: (i, 0)
    return pl.pallas_call(
        functools.partial(_router_kernel, n_experts=n_experts),
        out_shape=[jax.ShapeDtypeStruct((n, d), F32), jax.ShapeDtypeStruct((n, LANES), F32),
                   jax.ShapeDtypeStruct((SUBLANES, LANES), F32)],
        grid=(n // tm,),
        in_specs=[pl.BlockSpec((tm, d), row), pl.BlockSpec((1, 6, d), _seg_index_map(tm, n_ctx, seq)),
                  _const_spec((1, d)), _const_spec(wr.shape), _const_spec(tri.shape)],
        out_specs=[pl.BlockSpec((tm, d), row), pl.BlockSpec((tm, LANES), row),
                   pl.BlockSpec((SUBLANES, LANES), lambda i: (0, 0))],
        scratch_shapes=[pltpu.VMEM((SUBLANES, LANES), F32)],
        compiler_params=_cparams(("arbitrary",)),
        name="moe_router",
    )(x_all, mod, g, wr, tri)


def _dispatch_kernel(zrow_ref, pos_ref, h_ref, xg_ref, zero_scr, sem, zsem, *, n_zero):
    tm = h_ref.shape[0]
    tz = zero_scr.shape[0]

    def zero_copy(e):
        return pltpu.make_async_copy(zero_scr, xg_ref.at[pl.ds(pl.multiple_of(zrow_ref[e], tz), tz)], zsem)

    @pl.when(pl.program_id(0) == 0)
    def _():
        zero_scr[...] = jnp.zeros_like(zero_scr)
        for e in range(n_zero):
            @pl.when(zrow_ref[e] >= 0)
            def _():
                zero_copy(e).start()
        for e in range(n_zero):
            @pl.when(zrow_ref[e] >= 0)
            def _():
                zero_copy(e).wait()

    def issue(r, carry):
        for k in range(TOP_K):
            dst = pos_ref[k * tm + r]
            pltpu.make_async_copy(h_ref.at[pl.ds(r, 1)], xg_ref.at[pl.ds(dst, 1)], sem).start()
        return carry

    lax.fori_loop(0, tm, issue, 0, unroll=8)
    for k in range(TOP_K):
        pltpu.make_async_copy(h_ref, xg_ref.at[pl.ds(0, tm)], sem).wait()


def _dispatch(h_all, pos_tiles, zero_tile_row, n_rows):
    n, d = h_all.shape
    tm = TOKEN_TILE
    n_zero = zero_tile_row.shape[0]
    grid_spec = pltpu.PrefetchScalarGridSpec(
        num_scalar_prefetch=1,
        grid=(n // tm,),
        in_specs=[pl.BlockSpec((TOP_K * tm,), lambda i, last: (i,), memory_space=pltpu.SMEM),
                  pl.BlockSpec((tm, d), lambda i, last: (i, 0))],
        out_specs=pl.BlockSpec(memory_space=pl.ANY),
        scratch_shapes=[pltpu.VMEM((MOE_TILE, d), F32), pltpu.SemaphoreType.DMA(()), pltpu.SemaphoreType.DMA(())],
    )
    return pl.pallas_call(
        functools.partial(_dispatch_kernel, n_zero=n_zero),
        out_shape=jax.ShapeDtypeStruct((n_rows, d), F32),
        grid_spec=grid_spec,
        compiler_params=_cparams(("arbitrary",)),
        name="moe_dispatch",
    )(zero_tile_row, pos_tiles, h_all)


def _experts_kernel(te_ref, nt_ref, x_ref, w1_ref, w3_ref, w2_ref, o_ref, xb_scr):
    c = pl.program_id(1)

    @pl.when(pl.program_id(0) < nt_ref[0])
    def _():
        @pl.when(c == 0)
        def _():
            xb_scr[...] = x_ref[...].astype(BF16)
        xb = xb_scr[...]
        a = (_silu(_dot(xb, w1_ref[0])) * _dot(xb, w3_ref[0])).astype(BF16)
        y = _dot(a, w2_ref[0])

        @pl.when(c == 0)
        def _():
            o_ref[...] = y

        @pl.when(c > 0)
        def _():
            o_ref[...] += y

    @pl.when(jnp.logical_and(pl.program_id(0) >= nt_ref[0], c == 0))
    def _():
        o_ref[...] = jnp.zeros_like(o_ref)


def _experts(xg, w1, w3, w2, tile_expert, n_tiles_used):
    p, d = xg.shape
    tm = MOE_TILE
    ff = w1.shape[2]
    fc = MOE_FF_CHUNK if ff % MOE_FF_CHUNK == 0 else ff
    nc = ff // fc
    n_tiles = p // tm

    def tile(t, nt):
        return jnp.minimum(t, nt[0] - 1)

    def chunk(t, c, nt):
        return jnp.where(t < nt[0], c, nc - 1)

    grid_spec = pltpu.PrefetchScalarGridSpec(
        num_scalar_prefetch=2,
        grid=(n_tiles, nc),
        in_specs=[pl.BlockSpec((tm, d), lambda t, c, te, nt: (tile(t, nt), 0)),
                  pl.BlockSpec((1, d, fc), lambda t, c, te, nt: (te[tile(t, nt)], 0, chunk(t, c, nt))),
                  pl.BlockSpec((1, d, fc), lambda t, c, te, nt: (te[tile(t, nt)], 0, chunk(t, c, nt))),
                  pl.BlockSpec((1, fc, d), lambda t, c, te, nt: (te[tile(t, nt)], chunk(t, c, nt), 0))],
        out_specs=pl.BlockSpec((tm, d), lambda t, c, te, nt: (t, 0)),
        scratch_shapes=[pltpu.VMEM((tm, d), BF16)],
    )
    return pl.pallas_call(
        _experts_kernel,
        out_shape=jax.ShapeDtypeStruct((p, d), F32),
        grid_spec=grid_spec,
        compiler_params=_cparams(("arbitrary", "arbitrary")),
        name="moe_experts",
    )(tile_expert, n_tiles_used, xg, w1, w3, w2)


def _combine_kernel(pos_ref, yg_ref, x_ref, info_ref, mod_ref, fg_ref, o_ref, y_scr, sem):
    tm = x_ref.shape[0]

    def issue(r, carry):
        for k in range(TOP_K):
            src = pos_ref[k * tm + r]
            pltpu.make_async_copy(yg_ref.at[pl.ds(src, 1)], y_scr.at[k, pl.ds(r, 1)], sem).start()
        return carry

    lax.fori_loop(0, tm, issue, 0, unroll=8)
    for k in range(TOP_K):
        pltpu.make_async_copy(yg_ref.at[pl.ds(0, tm)], y_scr.at[k], sem).wait()
    info = info_ref[...]
    y = info[:, 4:5] * y_scr[0] + info[:, 5:6] * y_scr[1]
    x = x_ref[...] + mod_ref[0][5:6] * y
    o_ref[...] = x * lax.rsqrt(jnp.mean(x * x, axis=-1, keepdims=True) + NORM_EPS) * fg_ref[...]


def _combine(yg, x_all, info, pos_tiles, mod, final_g, *, row0, n_rows, seq, mod_row0):
    d = x_all.shape[1]
    tm = TOKEN_TILE
    t0 = row0 // tm
    per_seq = seq // tm
    grid_spec = pltpu.PrefetchScalarGridSpec(
        num_scalar_prefetch=0,
        grid=(n_rows // tm,),
        in_specs=[pl.BlockSpec((TOP_K * tm,), lambda i: (t0 + i,), memory_space=pltpu.SMEM),
                  pl.BlockSpec(memory_space=pl.ANY),
                  pl.BlockSpec((tm, d), lambda i: (t0 + i, 0)),
                  pl.BlockSpec((tm, LANES), lambda i: (t0 + i, 0)),
                  pl.BlockSpec((1, 6, d), lambda i: (mod_row0 + (i // per_seq if mod_row0 else 0), 0, 0)),
                  _const_spec((1, d))],
        out_specs=pl.BlockSpec((tm, d), lambda i: (i, 0)),
        scratch_shapes=[pltpu.VMEM((TOP_K, tm, d), F32), pltpu.SemaphoreType.DMA(())],
    )
    return pl.pallas_call(
        _combine_kernel,
        out_shape=jax.ShapeDtypeStruct((n_rows, d), F32),
        grid_spec=grid_spec,
        compiler_params=_cparams(("arbitrary",)),
        name="moe_combine",
    )(pos_tiles, yg, x_all, info, mod, final_g)


def _moe(x_all, mod, g, w_router, w1, w3, w2, final_g, *, n_ctx, seq_ctx_total, seq_lat):
    n, d = x_all.shape
    n_experts = w_router.shape[1]
    tm = TOKEN_TILE
    h_all, info, counts = _router(x_all, mod, g, w_router, n_ctx=n_ctx, seq=seq_lat)
    cnt = counts[0, :n_experts].astype(jnp.int32)
    padded = ((cnt + MOE_TILE - 1) // MOE_TILE) * MOE_TILE
    ends = jnp.cumsum(padded)
    offs = ends - padded
    n_tiles = (TOP_K * n) // MOE_TILE + n_experts
    n_used = (ends[-1] // MOE_TILE).astype(jnp.int32)
    tile_start = jnp.arange(n_tiles, dtype=jnp.int32) * MOE_TILE
    tile_expert = jnp.minimum(jnp.sum(tile_start[:, None] >= ends[None, :], axis=1), n_experts - 1).astype(jnp.int32)
    tile_expert = jnp.where(tile_start < ends[-1], tile_expert, tile_expert[jnp.maximum(n_used - 1, 0)])
    tail_start = tile_start[n_tiles - n_experts:]
    zero_tile_row = jnp.concatenate([jnp.where(padded > 0, ends - MOE_TILE, -1),
                                     jnp.where(tail_start >= ends[-1], tail_start, -1)]).astype(jnp.int32)
    e_idx = info[:, 0:TOP_K].astype(jnp.int32)
    pos = offs[e_idx] + info[:, TOP_K:2 * TOP_K].astype(jnp.int32)
    pos_tiles = pos.reshape(n // tm, tm, TOP_K).transpose(0, 2, 1).reshape(-1)
    xg = _dispatch(h_all, pos_tiles, zero_tile_row, n_tiles * MOE_TILE)
    yg = _experts(xg, w1, w3, w2, tile_expert, n_used.reshape(1))
    y_ctx = _combine(yg, x_all, info, pos_tiles, mod, final_g, row0=0, n_rows=n_ctx, seq=seq_ctx_total, mod_row0=0)
    y_lat = _combine(yg, x_all, info, pos_tiles, mod, final_g, row0=n_ctx, n_rows=n - n_ctx, seq=seq_lat, mod_row0=1)
    return y_ctx, y_lat


def _rope_tables(seq, hd):
    axis_dim = hd // 2
    nf = axis_dim // 2
    inv = jnp.power(ROPE_BASE, -jnp.arange(nf, dtype=F32) / nf)
    pos = np.arange(seq)
    row = jnp.asarray(pos // GRID_W, F32)
    col = jnp.asarray(pos % GRID_W, F32)
    lane = np.arange(2 * hd)
    dd = lane % hd
    use_col = (dd // axis_dim) == 1
    e = dd % axis_dim
    first = jnp.asarray(e < nf)[None, :]
    freq = inv[e % nf]
    ang = jnp.where(jnp.asarray(use_col)[None, :], col[:, None], row[:, None]) * freq[None, :]
    cos, sin = jnp.cos(ang), jnp.sin(ang)
    return cos, jnp.where(first, -sin, 0.0), jnp.where(first, 0.0, sin)


def _channel_dft(fw):
    g = FOURIER_GROUP_DIM
    ang = (2.0 * np.pi / g) * ((np.arange(g)[:, None] * np.arange(g)[None, :]) % g)
    eye = np.eye(fw // g)
    return jnp.asarray(np.concatenate([np.kron(eye, np.cos(ang)), np.kron(eye, np.sin(ang))], axis=1), BF16)


def kernel(x_prompt, x_sample, cache_k, cache_v, c, c_ctx, w_ada, b_ada, norm_g, final_g, w_in_ab, w_out_ab,
           diff_lambda, diff_subln, conv_w_pw1, conv_w_dw, conv_b_dw, conv_ln_g, conv_ln_b, conv_w_pw2,
           ffn_w1, ffn_w3, ffn_w2, moe_router, moe_w1, moe_w3, moe_w2):
    bp, lp, d = x_prompt.shape
    bs, ls, _ = x_sample.shape
    lc = cache_k.shape[2]
    n_ctx, n_lat = bp * lp, bs * ls
    n_all = n_ctx + n_lat
    dw = N_HEADS * cache_k.shape[-1]
    hd = cache_k.shape[-1] // 2
    fw = w_in_ab.shape[2] - 3 * dw

    rows = -(-(1 + bs) // SUBLANES) * SUBLANES
    cond = jnp.zeros((rows, d), F32).at[0].set(c_ctx).at[1:1 + bs].set(c)
    mod = _modulation(cond, w_ada, b_ada)

    lam_init = 0.8 - 0.6 * math.exp(-0.3 * 0)
    q_scale = (hd ** -0.5) * LOG2E
    w_in = w_in_ab[0].astype(BF16)
    bd = _channel_dft(fw)
    g00 = norm_g[0, 0].reshape(1, d)
    xp = x_prompt.reshape(n_ctx, d)
    xs = x_sample.reshape(n_lat, d)
    acas_p, q_p, k_p, v_p, kc_new, vc_new = _inproj(
        xp, mod[0], g00, w_in, bd, seq=lp, mod_row0=0, rope_tables=None, cache=True, q_scale=q_scale)
    acas_s, q_s, k_s, v_s = _inproj(
        xs, mod[0], g00, w_in, bd, seq=ls, mod_row0=1, rope_tables=_rope_tables(ls, hd), cache=False, q_scale=q_scale)
    four_p = _fourier_short(acas_p, lp)
    four_s = _fourier_long(acas_s, ls)
    lam_p = diff_lambda[0]
    sub_g = diff_subln[0].reshape(1, 2 * hd)
    attn_p = _attention(lam_p, sub_g, q_p, k_p, v_p, None, None, seq=lp, lam_init=lam_init)
    k_ctx = cache_k[:, 0].reshape(bs * lc, dw).astype(BF16)
    v_ctx = cache_v[:, 0].reshape(bs * lc, dw).astype(BF16)
    attn_s = _attention(lam_p, sub_g, q_s, k_s, v_s, k_ctx, v_ctx, seq=ls, lam_init=lam_init)
    w_out = w_out_ab[0].astype(BF16)
    x_all = _outproj((xp, four_p, attn_p), (xs, four_s, attn_s), w_out, mod[0], seq=ls)
    x_all = _ffn(x_all, mod[0], norm_g[0, 1].reshape(1, d), ffn_w1[0].astype(BF16), ffn_w3[0].astype(BF16),
                 ffn_w2[0].astype(BF16), n_ctx=n_ctx, seq=ls)

    u = _pw1(x_all, mod[1], norm_g[1, 0].reshape(1, d), conv_w_pw1[0].astype(BF16), n_ctx=n_ctx, seq=ls)
    x_all = _conv(u, x_all, mod[1], conv_w_dw[0], conv_b_dw[0].reshape(1, d), conv_ln_g[0].reshape(1, d),
                  conv_ln_b[0].reshape(1, d), conv_w_pw2[0].astype(BF16), n_ctx=n_ctx, seq_ctx=lp, seq_lat=ls)
    y_ctx, y_lat = _moe(x_all, mod[1], norm_g[1, 1].reshape(1, d), moe_router[0], moe_w1[0].astype(BF16),
                        moe_w3[0].astype(BF16), moe_w2[0].astype(BF16), final_g.reshape(1, d),
                        n_ctx=n_ctx, seq_ctx_total=n_ctx, seq_lat=ls)

    n_even = w_in_ab.shape[0]
    new_k = kc_new.reshape(bp, n_even, lp, N_HEADS, 2 * hd)
    new_v = vc_new.reshape(bp, n_even, lp, N_HEADS, 2 * hd)
    return (y_ctx.reshape(bp, lp, d), y_lat.reshape(bs, ls, d), new_k, new_v)
```

```python
import functools
import math

import numpy as np
import jax
import jax.numpy as jnp
from jax import lax
from jax.experimental import pallas as pl
from jax.experimental.pallas import tpu as pltpu

F32 = jnp.float32
BF16 = jnp.bfloat16

GRID_W = 64
ROPE_BASE = 10000.0
N_HEADS = 4
FOURIER_GROUP_DIM = 128
CONV_HALO = 16
TOP_K = 2
NORM_EPS = 1e-6
SUBLN_EPS = 1e-5
LANES = 128
SUBLANES = 8
LOG2E = 1.4426950408889634

TOKEN_TILE = 512
CONV_TILE = 256
CONV_ROW_CHUNK = 64
ATTN_Q_TILE = 512
ATTN_K_TILE = 512
ATTN_SCORE_BUFFERS = 3
FOURIER_ROW_TILE = 512
FOURIER_TABLE_ROWS = 256
MOE_TILE = 512
MOE_FF_CHUNK = 1792
RUN_ALIGN = SUBLANES
RUN_SIZES = tuple(TOKEN_TILE >> s for s in range((TOKEN_TILE // RUN_ALIGN).bit_length()))
VMEM_LIMIT = 56 * 1024 * 1024


def _local_rows(tile, n_experts):
    return -(-(TOP_K * tile + n_experts * (RUN_ALIGN - 1)) // LANES) * LANES


def _cparams(semantics, vmem=VMEM_LIMIT):
    return pltpu.CompilerParams(dimension_semantics=semantics, vmem_limit_bytes=vmem)


def _const_spec(shape):
    nd = len(shape)
    return pl.BlockSpec(shape, lambda *_: (0,) * nd, pipeline_mode=pl.Buffered(1))


def _dot(a, b):
    return jnp.dot(a, b, preferred_element_type=F32)


def _silu(x):
    return x * jax.nn.sigmoid(x)


def _norm_mod(x, g, shift, scale):
    ms = jnp.mean(x * x, axis=-1, keepdims=True)
    return (x * lax.rsqrt(ms + NORM_EPS)) * (g * (1.0 + scale)) + shift


def _seg_index_map(tile, n_ctx, seq):
    def index_map(i, *_):
        start = i * tile
        return (jnp.where(start < n_ctx, 0, 1 + (start - n_ctx) // seq), 0, 0)
    return index_map


def _mod_kernel(c_ref, w_ref, b_ref, o_ref):
    c = c_ref[...]
    o_ref[0] = _dot(_silu(c).astype(BF16), w_ref[0].astype(BF16)) + b_ref[0]


def _modulation(cond, w_ada, b_ada):
    depth, d, n6 = w_ada.shape
    rows = cond.shape[0]
    tn = 1536
    out = pl.pallas_call(
        _mod_kernel,
        out_shape=jax.ShapeDtypeStruct((depth, rows, n6), F32),
        grid=(depth, n6 // tn),
        in_specs=[pl.BlockSpec((rows, d), lambda l, j: (0, 0)),
                  pl.BlockSpec((1, d, tn), lambda l, j: (l, 0, j)),
                  pl.BlockSpec((1, 1, tn), lambda l, j: (l, 0, j))],
        out_specs=pl.BlockSpec((1, rows, tn), lambda l, j: (l, 0, j)),
        compiler_params=_cparams(("parallel", "parallel")),
        name="modulation",
    )(cond, w_ada, b_ada.reshape(depth, 1, n6))
    return out.reshape(depth, rows, 6, d)


def _rope(t, cos, sin_lo, sin_hi):
    outs = []
    for h in range(t.shape[1] // LANES):
        th = t[:, h * LANES:(h + 1) * LANES]
        outs.append(th * cos + pltpu.roll(th, LANES - 16, 1) * sin_lo + pltpu.roll(th, 16, 1) * sin_hi)
    return jnp.concatenate(outs, axis=-1)


def _inproj_kernel(*refs, rope, cache, q_scale):
    x_ref, mod_ref, g_ref, w_ref, bd_ref = refs[:5]
    pos = 5
    if rope:
        cos_ref, slo_ref, shi_ref = refs[pos:pos + 3]
        pos += 3
    acas_ref, q_ref, k_ref, v_ref = refs[pos:pos + 4]
    m = mod_ref[0]
    h = _norm_mod(x_ref[...], g_ref[...], m[0:1], m[1:2]).astype(BF16)
    fw = bd_ref.shape[0]
    dw = q_ref.shape[1]
    f = _dot(h, w_ref[:, 0:fw])
    acas_ref[...] = _dot(f.astype(BF16), bd_ref[...]).astype(BF16)
    q = _dot(h, w_ref[:, fw:fw + dw])
    k = _dot(h, w_ref[:, fw + dw:fw + 2 * dw])
    v = _dot(h, w_ref[:, fw + 2 * dw:fw + 3 * dw])
    if cache:
        kc_ref, vc_ref = refs[pos + 4:pos + 6]
        kc_ref[...] = k
        vc_ref[...] = v
    if rope:
        cos, slo, shi = cos_ref[...], slo_ref[...], shi_ref[...]
        q = _rope(q, cos, slo, shi)
        k = _rope(k, cos, slo, shi)
    q_ref[...] = (q * q_scale).astype(BF16)
    k_ref[...] = k.astype(BF16)
    v_ref[...] = v.astype(BF16)


def _inproj(x, mod, g, w_in, bd, *, seq, mod_row0, rope_tables, cache, q_scale):
    n, d = x.shape
    tm = min(TOKEN_TILE, seq)
    fw = bd.shape[0]
    dw = (w_in.shape[1] - fw) // 3
    per_seq = seq // tm
    row = lambda i: (i, 0)
    in_specs = [pl.BlockSpec((tm, d), row),
                pl.BlockSpec((1, 6, d), lambda i: (mod_row0 + (i // per_seq if mod_row0 else 0), 0, 0)),
                _const_spec((1, d)), _const_spec(w_in.shape), _const_spec(bd.shape)]
    args = [x, mod, g, w_in, bd]
    if rope_tables is not None:
        in_specs += [pl.BlockSpec((tm, LANES), lambda i: (i % per_seq, 0))] * 3
        args += list(rope_tables)
    out_shape = [jax.ShapeDtypeStruct((n, 2 * fw), BF16)] + [jax.ShapeDtypeStruct((n, dw), BF16)] * 3
    out_specs = [pl.BlockSpec((tm, 2 * fw), row)] + [pl.BlockSpec((tm, dw), row)] * 3
    if cache:
        out_shape += [jax.ShapeDtypeStruct((n, dw), F32)] * 2
        out_specs += [pl.BlockSpec((tm, dw), row)] * 2
    return pl.pallas_call(
        functools.partial(_inproj_kernel, rope=rope_tables is not None, cache=cache, q_scale=q_scale),
        out_shape=out_shape, grid=(n // tm,), in_specs=in_specs, out_specs=out_specs,
        compiler_params=_cparams(("parallel",)),
        name="inproj_ctx" if cache else "inproj_lat",
    )(*args)


def _fourier_short_kernel(ct_ref, st_ref, acas_ref, o_ref, *, scale):
    fw = o_ref.shape[1]
    y = _dot(ct_ref[...], acas_ref[:, 0:fw]) - _dot(st_ref[...], acas_ref[:, fw:2 * fw])
    o_ref[...] = (y * scale).astype(BF16)


def _fourier_short(acas, seq):
    n, fw2 = acas.shape
    fw = fw2 // 2
    idx = (np.arange(seq)[:, None] * np.arange(seq)[None, :]) % seq
    ang = (2.0 * np.pi / seq) * idx
    ct = jnp.asarray(np.cos(ang), BF16)
    st = jnp.asarray(np.sin(ang), BF16)
    scale = 1.0 / math.sqrt(seq * FOURIER_GROUP_DIM)
    return pl.pallas_call(
        functools.partial(_fourier_short_kernel, scale=scale),
        out_shape=jax.ShapeDtypeStruct((n, fw), BF16),
        grid=(n // seq,),
        in_specs=[_const_spec((seq, seq)), _const_spec((seq, seq)),
                  pl.BlockSpec((seq, fw2), lambda b: (b, 0))],
        out_specs=pl.BlockSpec((seq, fw), lambda b: (b, 0)),
        compiler_params=_cparams(("parallel",)),
        name="fourier_ctx",
    )(ct, st, acas)


def _fourier_long_kernel(cb_ref, sb_ref, ca_ref, sa_ref, acas_ref, o_ref, ct_scr, st_scr, *, scale):
    @pl.when(pl.program_id(1) == 0)
    def _():
        tb, seq = cb_ref.shape
        cw = 512
        for part in range(ca_ref.shape[0]):
            rows = slice(part * tb, (part + 1) * tb)
            for c0 in range(0, seq, cw):
                ca, sa = ca_ref[part, :, c0:c0 + cw], sa_ref[part, :, c0:c0 + cw]
                cb, sb = cb_ref[:, c0:c0 + cw], sb_ref[:, c0:c0 + cw]
                ct_scr[rows, c0:c0 + cw] = (ca * cb - sa * sb).astype(BF16)
                st_scr[rows, c0:c0 + cw] = (sa * cb + ca * sb).astype(BF16)
    fw = o_ref.shape[1]
    y = _dot(ct_scr[...], acas_ref[:, 0:fw]) - _dot(st_scr[...], acas_ref[:, fw:2 * fw])
    o_ref[...] = (y * scale).astype(BF16)


def _fourier_long(acas, seq):
    n, fw2 = acas.shape
    fw = fw2 // 2
    nb = n // seq
    tr = min(FOURIER_ROW_TILE, seq)
    tb = min(FOURIER_TABLE_ROWS, tr)
    parts = tr // tb
    nr = seq // tr
    col = np.arange(seq)[None, :]
    beta = (2.0 * np.pi / seq) * ((np.arange(tb)[:, None] * col) % seq)
    alpha = (2.0 * np.pi / seq) * (((np.arange(seq // tb) * tb)[:, None] * col) % seq)
    cb, sb = jnp.asarray(np.cos(beta), F32), jnp.asarray(np.sin(beta), F32)
    ca = jnp.asarray(np.cos(alpha), F32).reshape(seq // tb, 1, seq)
    sa = jnp.asarray(np.sin(alpha), F32).reshape(seq // tb, 1, seq)
    scale = 1.0 / math.sqrt(seq * FOURIER_GROUP_DIM)
    return pl.pallas_call(
        functools.partial(_fourier_long_kernel, scale=scale),
        out_shape=jax.ShapeDtypeStruct((n, fw), BF16),
        grid=(nr, nb),
        in_specs=[_const_spec((tb, seq)), _const_spec((tb, seq)),
                  pl.BlockSpec((parts, 1, seq), lambda r, b: (r, 0, 0)),
                  pl.BlockSpec((parts, 1, seq), lambda r, b: (r, 0, 0)),
                  pl.BlockSpec((seq, fw2), lambda r, b: (b, 0))],
        out_specs=pl.BlockSpec((tr, fw), lambda r, b: (b * nr + r, 0)),
        scratch_shapes=[pltpu.VMEM((tr, seq), BF16), pltpu.VMEM((tr, seq), BF16)],
        compiler_params=_cparams(("arbitrary", "arbitrary")),
        name="fourier_lat",
    )(cb, sb, ca, sa, acas)


def _col_reduce(x, op):
    r = x.reshape(x.shape[0] // SUBLANES, SUBLANES, x.shape[1])
    n = r.shape[0]
    while n > 1:
        n //= 2
        r = op(r[:n], r[n:2 * n])
    r = r[0]
    for shift in (4, 2, 1):
        r = op(r, pltpu.roll(r, shift, 0))
    return r[0:1]


def _attn_kernel(*refs, n_chunks, tk, has_ctx, lam_init):
    lam_ref, sub_ref, q_ref, k_ref, v_ref = refs[:5]
    pos = 5
    if has_ctx:
        kc_ref, vc_ref = refs[5:7]
        pos = 7
    o_ref, vt_scr, acc_scr, s_scr = refs[pos:pos + 4]
    if has_ctx:
        vtc_scr = refs[pos + 4]
    tq, hd2 = q_ref.shape
    half = hd2 // 2
    depth = s_scr.shape[0]

    @pl.when(pl.program_id(2) == 0)
    def _():
        for c in range(n_chunks):
            vt_scr[c] = v_ref[c * tk:(c + 1) * tk, :].astype(F32).T.astype(BF16)
        if has_ctx:
            vtc_scr[...] = vc_ref[...].astype(F32).T.astype(BF16)

    qt = q_ref[...].astype(F32).T
    row = lax.broadcasted_iota(jnp.int32, qt.shape, 0)
    qts = (jnp.where(row < half, qt, 0.0).astype(BF16),
           jnp.where(row >= half, qt, 0.0).astype(BF16))

    chunks = [(k_ref, c * tk, tk, functools.partial(vt_scr.__getitem__, c)) for c in range(n_chunks)]
    if has_ctx:
        chunks.append((kc_ref, 0, kc_ref.shape[0], functools.partial(vtc_scr.__getitem__, Ellipsis)))

    def scores(idx):
        k_src, start, size, _ = chunks[idx]
        kc = k_src[start:start + size, :]
        for j in range(2):
            s_scr[idx % depth, j, 0:size, :] = _dot(kc, qts[j])

    def softmax_pv(idx, state):
        _, _, size, vt = chunks[idx]
        new = []
        for j in range(2):
            s = s_scr[idx % depth, j, 0:size, :]
            m_blk = _col_reduce(s, jnp.maximum)
            if state is None:
                m_new = m_blk
                p = jnp.exp2(s - m_new)
                l_new = _col_reduce(p, jnp.add)
                acc_scr[j] = _dot(vt(), p.astype(BF16))
            else:
                m, l = state[j]
                m_new = jnp.maximum(m, m_blk)
                alpha = jnp.exp2(m - m_new)
                p = jnp.exp2(s - m_new)
                l_new = alpha * l + _col_reduce(p, jnp.add)
                acc_scr[j] = alpha * acc_scr[j] + _dot(vt(), p.astype(BF16))
            new.append((m_new, l_new))
        return tuple(new)

    state = None
    ahead = depth - 1
    for idx in range(min(ahead, len(chunks))):
        scores(idx)
    for idx in range(len(chunks)):
        if idx + ahead < len(chunks):
            scores(idx + ahead)
        state = softmax_pv(idx, state)
    (_, l1), (_, l2) = state
    lp = lam_ref[...]
    lam = (jnp.exp(jnp.sum(lp[0:1] * lp[1:2], axis=-1, keepdims=True))
           - jnp.exp(jnp.sum(lp[2:3] * lp[3:4], axis=-1, keepdims=True)) + lam_init)
    ot = acc_scr[0] * (1.0 / l1) - lam * (acc_scr[1] * (1.0 / l2))
    o = ot.T
    o = o * lax.rsqrt(jnp.mean(o * o, axis=-1, keepdims=True) + SUBLN_EPS)
    o_ref[...] = (o * sub_ref[...] * (1.0 - lam_init)).astype(BF16)


def _attention(lam_p, sub_g, q, k, v, k_ctx, v_ctx, *, seq, lam_init):
    n, dw = q.shape
    nb = n // seq
    hd2 = dw // N_HEADS
    tq = min(ATTN_Q_TILE, seq)
    tk = min(ATTN_K_TILE, seq)
    nq = seq // tq
    has_ctx = k_ctx is not None
    in_specs = [_const_spec(lam_p.shape), _const_spec(sub_g.shape),
                pl.BlockSpec((tq, hd2), lambda b, h, i: (b * nq + i, h)),
                pl.BlockSpec((seq, hd2), lambda b, h, i: (b, h)),
                pl.BlockSpec((seq, hd2), lambda b, h, i: (b, h))]
    args = [lam_p, sub_g, q, k, v]
    scratch = [pltpu.VMEM((seq // tk, hd2, tk), BF16), pltpu.VMEM((2, hd2, tq), F32),
               pltpu.VMEM((ATTN_SCORE_BUFFERS, 2, tk, tq), F32)]
    if has_ctx:
        lc = k_ctx.shape[0] // nb
        in_specs += [pl.BlockSpec((lc, hd2), lambda b, h, i: (b, h))] * 2
        args += [k_ctx, v_ctx]
        scratch.append(pltpu.VMEM((hd2, lc), BF16))
    return pl.pallas_call(
        functools.partial(_attn_kernel, n_chunks=seq // tk, tk=tk, has_ctx=has_ctx, lam_init=lam_init),
        out_shape=jax.ShapeDtypeStruct((n, dw), BF16),
        grid=(nb, N_HEADS, nq),
        in_specs=in_specs,
        out_specs=pl.BlockSpec((tq, hd2), lambda b, h, i: (b * nq + i, h)),
        scratch_shapes=scratch,
        compiler_params=_cparams(("parallel", "parallel", "arbitrary")),
        name="attn_lat" if has_ctx else "attn_ctx",
    )(*args)


def _outproj_kernel(xc_ref, fc_ref, ac_ref, xl_ref, fl_ref, al_ref, w_ref, mod_ref, o_ref, *, ctx_tiles):
    def residual(x_ref, four_ref, attn_ref):
        fw = four_ref.shape[1]
        y = _dot(four_ref[...], w_ref[0:fw, :]) + _dot(attn_ref[...], w_ref[fw:, :])
        o_ref[...] = x_ref[...] + mod_ref[0][2:3] * y

    @pl.when(pl.program_id(0) < ctx_tiles)
    def _():
        residual(xc_ref, fc_ref, ac_ref)

    @pl.when(pl.program_id(0) >= ctx_tiles)
    def _():
        residual(xl_ref, fl_ref, al_ref)


def _outproj(ctx, lat, w_out, mod, *, seq):
    n_ctx, d = ctx[0].shape
    n_lat = lat[0].shape[0]
    tm = TOKEN_TILE
    tc = n_ctx // tm
    ctx_row = lambda i: (jnp.minimum(i, tc - 1), 0)
    lat_row = lambda i: (jnp.maximum(i - tc, 0), 0)
    in_specs = ([pl.BlockSpec((tm, a.shape[1]), ctx_row) for a in ctx]
                + [pl.BlockSpec((tm, a.shape[1]), lat_row) for a in lat]
                + [_const_spec(w_out.shape), pl.BlockSpec((1, 6, d), _seg_index_map(tm, n_ctx, seq))])
    return pl.pallas_call(
        functools.partial(_outproj_kernel, ctx_tiles=tc),
        out_shape=jax.ShapeDtypeStruct((n_ctx + n_lat, d), F32),
        grid=((n_ctx + n_lat) // tm,), in_specs=in_specs,
        out_specs=pl.BlockSpec((tm, d), lambda i: (i, 0)),
        compiler_params=_cparams(("parallel",)),
        name="outproj",
    )(*ctx, *lat, w_out, mod)


def _ffn_kernel(x_ref, mod_ref, g_ref, w1_ref, w3_ref, w2_ref, o_ref):
    m = mod_ref[0]
    x = x_ref[...]
    h = _norm_mod(x, g_ref[...], m[3:4], m[4:5]).astype(BF16)
    a = (_silu(_dot(h, w1_ref[...])) * _dot(h, w3_ref[...])).astype(BF16)
    o_ref[...] = x + m[5:6] * _dot(a, w2_ref[...])


def _ffn(x_all, mod, g, w1, w3, w2, *, n_ctx, seq):
    n, d = x_all.shape
    tm = TOKEN_TILE
    row = lambda i: (i, 0)
    return pl.pallas_call(
        _ffn_kernel,
        out_shape=jax.ShapeDtypeStruct((n, d), F32),
        grid=(n // tm,),
        in_specs=[pl.BlockSpec((tm, d), row), pl.BlockSpec((1, 6, d), _seg_index_map(tm, n_ctx, seq)),
                  _const_spec((1, d)), _const_spec(w1.shape), _const_spec(w3.shape), _const_spec(w2.shape)],
        out_specs=pl.BlockSpec((tm, d), row),
        input_output_aliases={0: 0},
        compiler_params=_cparams(("parallel",)),
        name="ffn_dense",
    )(x_all, mod, g, w1, w3, w2)


def _pw1_kernel(x_ref, mod_ref, g_ref, w_ref, u_ref):
    m = mod_ref[0]
    h = _norm_mod(x_ref[...], g_ref[...], m[0:1], m[1:2]).astype(BF16)
    d = u_ref.shape[1]
    u_ref[...] = _dot(h, w_ref[:, 0:d]) * jax.nn.sigmoid(_dot(h, w_ref[:, d:2 * d]))


def _pw1(x_all, mod, g, w_pw1, *, n_ctx, seq):
    n, d = x_all.shape
    tm = TOKEN_TILE
    row = lambda i: (i, 0)
    return pl.pallas_call(
        _pw1_kernel,
        out_shape=jax.ShapeDtypeStruct((n, d), F32),
        grid=(n // tm,),
        in_specs=[pl.BlockSpec((tm, d), row), pl.BlockSpec((1, 6, d), _seg_index_map(tm, n_ctx, seq)),
                  _const_spec((1, d)), _const_spec(w_pw1.shape)],
        out_specs=pl.BlockSpec((tm, d), row),
        compiler_params=_cparams(("parallel",)),
        name="conv_pw1",
    )(x_all, mod, g, w_pw1)


def _conv_kernel(u_ref, prev_ref, next_ref, wdw_ref, bdw_ref, lng_ref, lnb_ref, w2_ref, x_ref, mod_ref,
                 o_ref, win_scr, conv_scr, shift_scr, *, n_ctx, seq_ctx, seq_lat, taps, front):
    tm, d = u_ref.shape
    start = pl.program_id(0) * tm
    is_ctx = start < n_ctx
    off = jnp.where(is_ctx, start % seq_ctx, (start - n_ctx) % seq_lat)
    slen = jnp.where(is_ctx, seq_ctx, seq_lat)
    win_scr[0:CONV_HALO, :] = jnp.where(off == 0, 0.0, prev_ref[...])
    win_scr[CONV_HALO:CONV_HALO + tm, :] = u_ref[...]
    win_scr[CONV_HALO + tm:, :] = jnp.where(off + tm == slen, 0.0, next_ref[...])
    n_q = wdw_ref.shape[0]
    rc = CONV_ROW_CHUNK
    for cb in range(d // LANES):
        cols = slice(cb * LANES, (cb + 1) * LANES)
        for k in range(1, SUBLANES):
            shift_scr[k] = win_scr[k:k + tm + n_q - SUBLANES, cols]
        w = {q: wdw_ref[q, :, cols] for q in range(front, front + taps)}
        bias = bdw_ref[:, cols]
        for r0 in range(0, tm, rc):
            acc = None
            for q in range(front, front + taps):
                k = q % SUBLANES
                if k:
                    tap = shift_scr[k, r0 + q - k:r0 + q - k + rc, :]
                else:
                    tap = win_scr[r0 + q:r0 + q + rc, cols]
                term = tap.reshape(rc // SUBLANES, SUBLANES, LANES) * w[q][None]
                acc = term if acc is None else acc + term
            conv_scr[r0:r0 + rc, cols] = acc.reshape(rc, LANES) + bias
    c = conv_scr[...]
    mu = jnp.mean(c, axis=-1, keepdims=True)
    cc = c - mu
    var = jnp.mean(cc * cc, axis=-1, keepdims=True)
    y = _silu(cc * lax.rsqrt(var + NORM_EPS) * lng_ref[...] + lnb_ref[...]).astype(BF16)
    o_ref[...] = x_ref[...] + mod_ref[0][2:3] * _dot(y, w2_ref[...])


def _conv(u, x_all, mod, w_dw, b_dw, ln_g, ln_b, w_pw2, *, n_ctx, seq_ctx, seq_lat):
    n, d = u.shape
    tm = min(CONV_TILE, seq_ctx)
    taps = w_dw.shape[0]
    front = CONV_HALO - taps // 2
    w_dw = jnp.pad(w_dw, ((front, 2 * CONV_HALO - taps - front), (0, 0)))
    w_dw = jnp.broadcast_to(w_dw[:, None, :], (2 * CONV_HALO, SUBLANES, d))
    hb = tm // CONV_HALO
    last = n // CONV_HALO - 1
    row = lambda i: (i, 0)
    return pl.pallas_call(
        functools.partial(_conv_kernel, n_ctx=n_ctx, seq_ctx=seq_ctx, seq_lat=seq_lat, taps=taps, front=front),
        out_shape=jax.ShapeDtypeStruct((n, d), F32),
        grid=(n // tm,),
        in_specs=[pl.BlockSpec((tm, d), row),
                  pl.BlockSpec((CONV_HALO, d), lambda i: (jnp.maximum(i * hb - 1, 0), 0)),
                  pl.BlockSpec((CONV_HALO, d), lambda i: (jnp.minimum((i + 1) * hb, last), 0)),
                  _const_spec(w_dw.shape), _const_spec((1, d)), _const_spec((1, d)), _const_spec((1, d)),
                  _const_spec(w_pw2.shape),
                  pl.BlockSpec((tm, d), row),
                  pl.BlockSpec((1, 6, d), _seg_index_map(tm, n_ctx, seq_lat))],
        out_specs=pl.BlockSpec((tm, d), row),
        scratch_shapes=[pltpu.VMEM((tm + 2 * CONV_HALO, d), F32), pltpu.VMEM((tm, d), F32),
                        pltpu.VMEM((SUBLANES, tm + 2 * CONV_HALO - SUBLANES, LANES), F32)],
        input_output_aliases={8: 0},
        compiler_params=_cparams(("parallel",)),
        name="conv_dw_pw2",
    )(u, u, u, w_dw, b_dw, ln_g, ln_b, w_pw2, x_all, mod)


def _router_kernel(x_ref, mod_ref, g_ref, wr_ref, tri_ref, upper_ref, h_ref, info_ref, runs_ref, *, n_experts):
    m = mod_ref[0]
    h = _norm_mod(x_ref[...], g_ref[...], m[3:4], m[4:5])
    h_hi = h.astype(BF16)
    h_ref[...] = h_hi
    h_lo = (h - h_hi.astype(F32)).astype(BF16)
    wr = wr_ref[...]
    w_hi = wr.astype(BF16)
    w_lo = (wr - w_hi.astype(F32)).astype(BF16)
    logits = _dot(h_hi, w_hi) + (_dot(h_lo, w_hi) + _dot(h_hi, w_lo))
    lane = lax.broadcasted_iota(jnp.int32, logits.shape, 1)
    neg = jnp.float32(-jnp.inf)
    logits = jnp.where(lane < n_experts, logits, neg)
    m1 = jnp.max(logits, axis=-1, keepdims=True)
    i1 = jnp.min(jnp.where(logits == m1, lane, LANES), axis=-1, keepdims=True)
    oh1 = lane == i1
    rest = jnp.where(oh1, neg, logits)
    m2 = jnp.max(rest, axis=-1, keepdims=True)
    i2 = jnp.min(jnp.where(rest == m2, lane, LANES), axis=-1, keepdims=True)
    oh2 = lane == i2
    e = jnp.exp(m2 - m1)
    g1 = 1.0 / (1.0 + e)
    g2 = e * g1
    oh = jnp.where(oh1, 1.0, 0.0) + jnp.where(oh2, 1.0, 0.0)
    cnt = jnp.sum(oh, axis=0, keepdims=True)
    run_units = jnp.floor((cnt + (RUN_ALIGN - 1.0)) * (1.0 / RUN_ALIGN))
    units8 = jnp.broadcast_to(run_units, (SUBLANES, LANES)).astype(BF16)
    run_start = _dot(units8, upper_ref[...])[0:1, :] * RUN_ALIGN
    before = _dot(tri_ref[...], oh.astype(BF16)) + run_start
    p1 = jnp.sum(jnp.where(oh1, before, 0.0), axis=-1, keepdims=True)
    p2 = jnp.sum(jnp.where(oh2, before, 0.0), axis=-1, keepdims=True)
    info = jnp.where(lane == 0, p1,
           jnp.where(lane == 1, p2,
           jnp.where(lane == 2, g1,
           jnp.where(lane == 3, g2, 0.0))))
    info_ref[...] = info
    sub = lax.broadcasted_iota(jnp.int32, (SUBLANES, LANES), 0)
    runs_ref[0] = jnp.where(sub == 0, run_units * RUN_ALIGN, jnp.where(sub == 1, run_start, 0.0))


def _router(x_all, mod, g, w_router, *, n_ctx, seq):
    n, d = x_all.shape
    n_experts = w_router.shape[1]
    tm = TOKEN_TILE
    wr = jnp.pad(w_router, ((0, 0), (0, LANES - n_experts)))
    tri = jnp.asarray(np.tril(np.ones((tm, tm), np.float32), -1), BF16)
    upper = jnp.asarray(np.triu(np.ones((LANES, LANES), np.float32), 1), BF16)
    row = lambda i: (i, 0)
    return pl.pallas_call(
        functools.partial(_router_kernel, n_experts=n_experts),
        out_shape=[jax.ShapeDtypeStruct((n, d), BF16), jax.ShapeDtypeStruct((n, LANES), F32),
                   jax.ShapeDtypeStruct((n // tm, SUBLANES, LANES), F32)],
        grid=(n // tm,),
        in_specs=[pl.BlockSpec((tm, d), row), pl.BlockSpec((1, 6, d), _seg_index_map(tm, n_ctx, seq)),
                  _const_spec((1, d)), _const_spec(wr.shape), _const_spec(tri.shape), _const_spec(upper.shape)],
        out_specs=[pl.BlockSpec((tm, d), row), pl.BlockSpec((tm, LANES), row),
                   pl.BlockSpec((1, SUBLANES, LANES), lambda i: (i, 0, 0))],
        compiler_params=_cparams(("parallel",)),
        name="moe_router",
    )(x_all, mod, g, wr, tri, upper)


def _run_copies(n_experts, len_ref, start_ref, dst_ref, tile, make_copy):
    ops = []
    for e in range(n_experts):
        n = len_ref[tile * n_experts + e]
        src = start_ref[tile * n_experts + e]
        dst = dst_ref[tile * n_experts + e]
        for size in RUN_SIZES:
            off = n & ~(2 * size - 1)
            ops.append(((n & size) != 0,
                        make_copy(pl.multiple_of(src + off, RUN_ALIGN), pl.multiple_of(dst + off, RUN_ALIGN), size)))
    return ops


def _dispatch_kernel(zrow_ref, len_ref, start_ref, dst_ref, h_ref, info_ref, xg_ref, zero_scr, xs_scr, sem, zsem,
                     *, n_zero, n_experts):
    tm = h_ref.shape[0]
    tz = zero_scr.shape[0]
    rows = xs_scr.shape[0]

    def zero_copy(e):
        return pltpu.make_async_copy(zero_scr, xg_ref.at[pl.ds(pl.multiple_of(zrow_ref[e], tz), tz)], zsem)

    @pl.when(pl.program_id(0) == 0)
    def _():
        zero_scr[...] = jnp.zeros_like(zero_scr)
        for e in range(n_zero):
            @pl.when(zrow_ref[e] >= 0)
            def _():
                zero_copy(e).start()
        for e in range(n_zero):
            @pl.when(zrow_ref[e] >= 0)
            def _():
                zero_copy(e).wait()

    info_t = info_ref[...].T
    slot1 = info_t[0:1, :].astype(jnp.int32)
    slot2 = info_t[1:2, :].astype(jnp.int32)
    r = lax.broadcasted_iota(jnp.int32, (rows, tm), 0)
    perm = (jnp.where(r == slot1, 1.0, 0.0) + jnp.where(r == slot2, 1.0, 0.0)).astype(BF16)
    xs_scr[...] = _dot(perm, h_ref[...])

    def make_copy(src, dst, size):
        return pltpu.make_async_copy(xs_scr.at[pl.ds(src, size)], xg_ref.at[pl.ds(dst, size)], sem)

    copies = _run_copies(n_experts, len_ref, start_ref, dst_ref, pl.program_id(0), make_copy)
    for cond, copy in copies:
        pl.when(cond)(copy.start)
    for cond, copy in copies:
        pl.when(cond)(copy.wait)


def _dispatch(h_all, info, run_len, run_start, run_dst, zero_tile_row, n_rows):
    n, d = h_all.shape
    tm = TOKEN_TILE
    n_zero = zero_tile_row.shape[0]
    n_experts = run_len.shape[0] // (n // tm)
    grid_spec = pltpu.PrefetchScalarGridSpec(
        num_scalar_prefetch=4,
        grid=(n // tm,),
        in_specs=[pl.BlockSpec((tm, d), lambda i, *_: (i, 0)),
                  pl.BlockSpec((tm, LANES), lambda i, *_: (i, 0))],
        out_specs=pl.BlockSpec(memory_space=pl.ANY),
        scratch_shapes=[pltpu.VMEM((MOE_TILE, d), F32), pltpu.VMEM((_local_rows(tm, n_experts), d), F32),
                        pltpu.SemaphoreType.DMA(()), pltpu.SemaphoreType.DMA(())],
    )
    return pl.pallas_call(
        functools.partial(_dispatch_kernel, n_zero=n_zero, n_experts=n_experts),
        out_shape=jax.ShapeDtypeStruct((n_rows, d), F32),
        grid_spec=grid_spec,
        compiler_params=_cparams(("arbitrary",)),
        name="moe_dispatch",
    )(zero_tile_row, run_len, run_start, run_dst, h_all, info)


def _experts_kernel(te_ref, nt_ref, x_ref, w1_ref, w3_ref, w2_ref, o_ref, xb_scr):
    c = pl.program_id(1)

    @pl.when(pl.program_id(0) < nt_ref[0])
    def _():
        @pl.when(c == 0)
        def _():
            xb_scr[...] = x_ref[...].astype(BF16)
        xb = xb_scr[...]
        a = (_silu(_dot(xb, w1_ref[0])) * _dot(xb, w3_ref[0])).astype(BF16)
        y = _dot(a, w2_ref[0])

        @pl.when(c == 0)
        def _():
            o_ref[...] = y

        @pl.when(c > 0)
        def _():
            o_ref[...] += y

    @pl.when(jnp.logical_and(pl.program_id(0) >= nt_ref[0], c == 0))
    def _():
        o_ref[...] = jnp.zeros_like(o_ref)


def _experts(xg, w1, w3, w2, tile_expert, n_tiles_used):
    p, d = xg.shape
    tm = MOE_TILE
    ff = w1.shape[2]
    fc = MOE_FF_CHUNK if ff % MOE_FF_CHUNK == 0 else ff
    nc = ff // fc
    n_tiles = p // tm

    def tile(t, nt):
        return jnp.minimum(t, nt[0] - 1)

    def chunk(t, c, nt):
        return jnp.where(t < nt[0], c, nc - 1)

    grid_spec = pltpu.PrefetchScalarGridSpec(
        num_scalar_prefetch=2,
        grid=(n_tiles, nc),
        in_specs=[pl.BlockSpec((tm, d), lambda t, c, te, nt: (tile(t, nt), 0)),
                  pl.BlockSpec((1, d, fc), lambda t, c, te, nt: (te[tile(t, nt)], 0, chunk(t, c, nt))),
                  pl.BlockSpec((1, d, fc), lambda t, c, te, nt: (te[tile(t, nt)], 0, chunk(t, c, nt))),
                  pl.BlockSpec((1, fc, d), lambda t, c, te, nt: (te[tile(t, nt)], chunk(t, c, nt), 0))],
        out_specs=pl.BlockSpec((tm, d), lambda t, c, te, nt: (t, 0)),
        scratch_shapes=[pltpu.VMEM((tm, d), BF16)],
    )
    return pl.pallas_call(
        _experts_kernel,
        out_shape=jax.ShapeDtypeStruct((p, d), F32),
        grid_spec=grid_spec,
        compiler_params=_cparams(("arbitrary", "arbitrary")),
        name="moe_experts",
    )(tile_expert, n_tiles_used, xg, w1, w3, w2)


def _combine_kernel(len_ref, start_ref, dst_ref, yg_ref, x_ref, info_ref, mod_ref, fg_ref, o_ref, ys_scr, sem,
                    *, n_experts, tile0):
    tm = x_ref.shape[0]
    rows = ys_scr.shape[0]

    @pl.when(pl.program_id(0) == 0)
    def _():
        ys_scr[...] = jnp.zeros_like(ys_scr)

    def make_copy(src, dst, size):
        return pltpu.make_async_copy(yg_ref.at[pl.ds(dst, size)], ys_scr.at[pl.ds(src, size)], sem)

    copies = _run_copies(n_experts, len_ref, start_ref, dst_ref, tile0 + pl.program_id(0), make_copy)
    for cond, copy in copies:
        pl.when(cond)(copy.start)
    for cond, copy in copies:
        pl.when(cond)(copy.wait)
    info = info_ref[...]
    ys = ys_scr[...].astype(BF16)
    lane = lax.broadcasted_iota(jnp.int32, (tm, rows), 1)
    pick1 = jnp.where(lane == info[:, 0:1].astype(jnp.int32), 1.0, 0.0).astype(BF16)
    pick2 = jnp.where(lane == info[:, 1:2].astype(jnp.int32), 1.0, 0.0).astype(BF16)
    y = info[:, 2:3] * _dot(pick1, ys) + info[:, 3:4] * _dot(pick2, ys)
    x = x_ref[...] + mod_ref[0][5:6] * y
    o_ref[...] = x * lax.rsqrt(jnp.mean(x * x, axis=-1, keepdims=True) + NORM_EPS) * fg_ref[...]


def _combine(yg, x_all, info, run_len, run_start, run_dst, mod, final_g, *, row0, n_rows, seq, mod_row0):
    n, d = x_all.shape
    tm = TOKEN_TILE
    t0 = row0 // tm
    per_seq = seq // tm
    n_experts = run_len.shape[0] // (n // tm)
    grid_spec = pltpu.PrefetchScalarGridSpec(
        num_scalar_prefetch=3,
        grid=(n_rows // tm,),
        in_specs=[pl.BlockSpec(memory_space=pl.ANY),
                  pl.BlockSpec((tm, d), lambda i, *_: (t0 + i, 0)),
                  pl.BlockSpec((tm, LANES), lambda i, *_: (t0 + i, 0)),
                  pl.BlockSpec((1, 6, d), lambda i, *_: (mod_row0 + (i // per_seq if mod_row0 else 0), 0, 0)),
                  pl.BlockSpec((1, d), lambda i, *_: (0, 0))],
        out_specs=pl.BlockSpec((tm, d), lambda i, *_: (i, 0)),
        scratch_shapes=[pltpu.VMEM((_local_rows(tm, n_experts), d), F32), pltpu.SemaphoreType.DMA(())],
    )
    return pl.pallas_call(
        functools.partial(_combine_kernel, n_experts=n_experts, tile0=t0),
        out_shape=jax.ShapeDtypeStruct((n_rows, d), F32),
        grid_spec=grid_spec,
        compiler_params=_cparams(("arbitrary",)),
        name="moe_combine",
    )(run_len, run_start, run_dst, yg, x_all, info, mod, final_g)


def _moe(x_all, mod, g, w_router, w1, w3, w2, final_g, *, n_ctx, seq_ctx_total, seq_lat):
    n, d = x_all.shape
    n_experts = w_router.shape[1]
    tm = TOKEN_TILE
    h_all, info, runs = _router(x_all, mod, g, w_router, n_ctx=n_ctx, seq=seq_lat)
    run_len = runs[:, 0, :n_experts].astype(jnp.int32)
    run_start = runs[:, 1, :n_experts].astype(jnp.int32)
    cnt = jnp.sum(run_len, axis=0)
    padded = ((cnt + MOE_TILE - 1) // MOE_TILE) * MOE_TILE
    ends = jnp.cumsum(padded)
    offs = ends - padded
    run_dst = offs[None, :] + jnp.cumsum(run_len, axis=0) - run_len
    n_tiles = -(-(TOP_K * n + (n // tm) * n_experts * (RUN_ALIGN - 1)) // MOE_TILE) + n_experts
    n_used = (ends[-1] // MOE_TILE).astype(jnp.int32)
    tile_start = jnp.arange(n_tiles, dtype=jnp.int32) * MOE_TILE
    tile_expert = jnp.minimum(jnp.sum(tile_start[:, None] >= ends[None, :], axis=1), n_experts - 1).astype(jnp.int32)
    tile_expert = jnp.where(tile_start < ends[-1], tile_expert, tile_expert[jnp.maximum(n_used - 1, 0)])
    tail_start = tile_start[(TOP_K * n) // MOE_TILE:]
    zero_tile_row = jnp.concatenate([jnp.where(padded > 0, ends - MOE_TILE, -1),
                                     jnp.where(tail_start >= ends[-1], tail_start, -1)]).astype(jnp.int32)
    run_args = (run_len.reshape(-1), run_start.reshape(-1), run_dst.reshape(-1).astype(jnp.int32))
    xg = _dispatch(h_all, info, *run_args, zero_tile_row, n_tiles * MOE_TILE)
    yg = _experts(xg, w1, w3, w2, tile_expert, n_used.reshape(1))
    y_ctx = _combine(yg, x_all, info, *run_args, mod, final_g, row0=0, n_rows=n_ctx, seq=seq_ctx_total, mod_row0=0)
    y_lat = _combine(yg, x_all, info, *run_args, mod, final_g, row0=n_ctx, n_rows=n - n_ctx, seq=seq_lat,
                     mod_row0=1)
    return y_ctx, y_lat


def _rope_tables(seq, hd):
    axis_dim = hd // 2
    nf = axis_dim // 2
    inv = jnp.power(ROPE_BASE, -jnp.arange(nf, dtype=F32) / nf)
    pos = np.arange(seq)
    row = jnp.asarray(pos // GRID_W, F32)
    col = jnp.asarray(pos % GRID_W, F32)
    lane = np.arange(2 * hd)
    dd = lane % hd
    use_col = (dd // axis_dim) == 1
    e = dd % axis_dim
    first = jnp.asarray(e < nf)[None, :]
    freq = inv[e % nf]
    ang = jnp.where(jnp.asarray(use_col)[None, :], col[:, None], row[:, None]) * freq[None, :]
    cos, sin = jnp.cos(ang), jnp.sin(ang)
    return cos, jnp.where(first, -sin, 0.0), jnp.where(first, 0.0, sin)


def _channel_dft(fw):
    g = FOURIER_GROUP_DIM
    ang = (2.0 * np.pi / g) * ((np.arange(g)[:, None] * np.arange(g)[None, :]) % g)
    eye = np.eye(fw // g)
    return jnp.asarray(np.concatenate([np.kron(eye, np.cos(ang)), np.kron(eye, np.sin(ang))], axis=1), BF16)


def kernel(x_prompt, x_sample, cache_k, cache_v, c, c_ctx, w_ada, b_ada, norm_g, final_g, w_in_ab, w_out_ab,
           diff_lambda, diff_subln, conv_w_pw1, conv_w_dw, conv_b_dw, conv_ln_g, conv_ln_b, conv_w_pw2,
           ffn_w1, ffn_w3, ffn_w2, moe_router, moe_w1, moe_w3, moe_w2):
    bp, lp, d = x_prompt.shape
    bs, ls, _ = x_sample.shape
    lc = cache_k.shape[2]
    n_ctx, n_lat = bp * lp, bs * ls
    n_all = n_ctx + n_lat
    dw = N_HEADS * cache_k.shape[-1]
    hd = cache_k.shape[-1] // 2
    fw = w_in_ab.shape[2] - 3 * dw

    rows = -(-(1 + bs) // SUBLANES) * SUBLANES
    cond = jnp.zeros((rows, d), F32).at[0].set(c_ctx).at[1:1 + bs].set(c)
    mod = _modulation(cond, w_ada, b_ada)

    lam_init = 0.8 - 0.6 * math.exp(-0.3 * 0)
    q_scale = (hd ** -0.5) * LOG2E
    w_in = w_in_ab[0].astype(BF16)
    bd = _channel_dft(fw)
    g00 = norm_g[0, 0].reshape(1, d)
    xp = x_prompt.reshape(n_ctx, d)
    xs = x_sample.reshape(n_lat, d)
    acas_p, q_p, k_p, v_p, kc_new, vc_new = _inproj(
        xp, mod[0], g00, w_in, bd, seq=lp, mod_row0=0, rope_tables=None, cache=True, q_scale=q_scale)
    acas_s, q_s, k_s, v_s = _inproj(
        xs, mod[0], g00, w_in, bd, seq=ls, mod_row0=1, rope_tables=_rope_tables(ls, hd), cache=False, q_scale=q_scale)
    four_p = _fourier_short(acas_p, lp)
    four_s = _fourier_long(acas_s, ls)
    lam_p = diff_lambda[0]
    sub_g = diff_subln[0].reshape(1, 2 * hd)
    attn_p = _attention(lam_p, sub_g, q_p, k_p, v_p, None, None, seq=lp, lam_init=lam_init)
    k_ctx = cache_k[:, 0].reshape(bs * lc, dw).astype(BF16)
    v_ctx = cache_v[:, 0].reshape(bs * lc, dw).astype(BF16)
    attn_s = _attention(lam_p, sub_g, q_s, k_s, v_s, k_ctx, v_ctx, seq=ls, lam_init=lam_init)
    w_out = w_out_ab[0].astype(BF16)
    x_all = _outproj((xp, four_p, attn_p), (xs, four_s, attn_s), w_out, mod[0], seq=ls)
    x_all = _ffn(x_all, mod[0], norm_g[0, 1].reshape(1, d), ffn_w1[0].astype(BF16), ffn_w3[0].astype(BF16),
                 ffn_w2[0].astype(BF16), n_ctx=n_ctx, seq=ls)

    u = _pw1(x_all, mod[1], norm_g[1, 0].reshape(1, d), conv_w_pw1[0].astype(BF16), n_ctx=n_ctx, seq=ls)
    x_all = _conv(u, x_all, mod[1], conv_w_dw[0], conv_b_dw[0].reshape(1, d), conv_ln_g[0].reshape(1, d),
                  conv_ln_b[0].reshape(1, d), conv_w_pw2[0].astype(BF16), n_ctx=n_ctx, seq_ctx=lp, seq_lat=ls)
    y_ctx, y_lat = _moe(x_all, mod[1], norm_g[1, 1].reshape(1, d), moe_router[0], moe_w1[0].astype(BF16),
                        moe_w3[0].astype(BF16), moe_w2[0].astype(BF16), final_g.reshape(1, d),
                        n_ctx=n_ctx, seq_ctx_total=n_ctx, seq_lat=ls)

    n_even = w_in_ab.shape[0]
    new_k = kc_new.reshape(bp, n_even, lp, N_HEADS, 2 * hd)
    new_v = vc_new.reshape(bp, n_even, lp, N_HEADS, 2 * hd)
    return (y_ctx.reshape(bp, lp, d), y_lat.reshape(bs, ls, d), new_k, new_v)
```

```python
import functools
import math

import numpy as np
import jax
import jax.numpy as jnp
from jax import lax
from jax.experimental import pallas as pl
from jax.experimental.pallas import tpu as pltpu

F32 = jnp.float32
BF16 = jnp.bfloat16

GRID_W = 64
ROPE_BASE = 10000.0
N_HEADS = 4
FOURIER_GROUP_DIM = 128
CONV_HALO = 16
TOP_K = 2
NORM_EPS = 1e-6
SUBLN_EPS = 1e-5
LANES = 128
SUBLANES = 8
LOG2E = 1.4426950408889634

TOKEN_TILE = 512
CONV_TILE = 256
CONV_ROW_CHUNK = 64
ATTN_Q_TILE = 512
ATTN_K_TILE = 512
ATTN_SCORE_BUFFERS = 2
FOURIER_ROW_TILE = 512
FOURIER_TABLE_ROWS = 256
MOE_TILE = 512
MOE_FF_CHUNK = 1792
RUN_ALIGN = SUBLANES
VMEM_LIMIT = 56 * 1024 * 1024


def _local_rows(tile, n_experts):
    return -(-(TOP_K * tile + n_experts * (RUN_ALIGN - 1)) // LANES) * LANES


def _cparams(semantics, vmem=VMEM_LIMIT):
    return pltpu.CompilerParams(dimension_semantics=semantics, vmem_limit_bytes=vmem)


def _const_spec(shape):
    nd = len(shape)
    return pl.BlockSpec(shape, lambda *_: (0,) * nd, pipeline_mode=pl.Buffered(1))


def _dot(a, b):
    return jnp.dot(a, b, preferred_element_type=F32)


def _silu(x):
    return x * jax.nn.sigmoid(x)


def _norm_mod(x, g, shift, scale):
    ms = jnp.mean(x * x, axis=-1, keepdims=True)
    return (x * lax.rsqrt(ms + NORM_EPS)) * (g * (1.0 + scale)) + shift


def _seg_index_map(tile, n_ctx, seq):
    def index_map(i, *_):
        start = i * tile
        return (jnp.where(start < n_ctx, 0, 1 + (start - n_ctx) // seq), 0, 0)
    return index_map


def _mod_kernel(c_ref, w_ref, b_ref, o_ref):
    c = c_ref[...]
    o_ref[0] = _dot(_silu(c).astype(BF16), w_ref[0].astype(BF16)) + b_ref[0]


def _modulation(cond, w_ada, b_ada):
    depth, d, n6 = w_ada.shape
    rows = cond.shape[0]
    tn = 1536
    out = pl.pallas_call(
        _mod_kernel,
        out_shape=jax.ShapeDtypeStruct((depth, rows, n6), F32),
        grid=(depth, n6 // tn),
        in_specs=[pl.BlockSpec((rows, d), lambda l, j: (0, 0)),
                  pl.BlockSpec((1, d, tn), lambda l, j: (l, 0, j)),
                  pl.BlockSpec((1, 1, tn), lambda l, j: (l, 0, j))],
        out_specs=pl.BlockSpec((1, rows, tn), lambda l, j: (l, 0, j)),
        compiler_params=_cparams(("parallel", "parallel")),
        name="modulation",
    )(cond, w_ada, b_ada.reshape(depth, 1, n6))
    return out.reshape(depth, rows, 6, d)


def _rope(t, cos, sin_lo, sin_hi):
    outs = []
    for h in range(t.shape[1] // LANES):
        th = t[:, h * LANES:(h + 1) * LANES]
        outs.append(th * cos + pltpu.roll(th, LANES - 16, 1) * sin_lo + pltpu.roll(th, 16, 1) * sin_hi)
    return jnp.concatenate(outs, axis=-1)


def _inproj_kernel(*refs, rope, cache, q_scale):
    x_ref, mod_ref, g_ref, w_ref, bd_ref = refs[:5]
    pos = 5
    if rope:
        cos_ref, slo_ref, shi_ref = refs[pos:pos + 3]
        pos += 3
    acas_ref, q_ref, k_ref, v_ref = refs[pos:pos + 4]
    m = mod_ref[0]
    h = _norm_mod(x_ref[...], g_ref[...], m[0:1], m[1:2]).astype(BF16)
    fw = bd_ref.shape[0]
    dw = q_ref.shape[1]
    f = _dot(h, w_ref[:, 0:fw])
    acas_ref[...] = _dot(f.astype(BF16), bd_ref[...]).astype(BF16)
    q = _dot(h, w_ref[:, fw:fw + dw])
    k = _dot(h, w_ref[:, fw + dw:fw + 2 * dw])
    v = _dot(h, w_ref[:, fw + 2 * dw:fw + 3 * dw])
    if cache:
        kc_ref, vc_ref = refs[pos + 4:pos + 6]
        kc_ref[...] = k
        vc_ref[...] = v
    if rope:
        cos, slo, shi = cos_ref[...], slo_ref[...], shi_ref[...]
        q = _rope(q, cos, slo, shi)
        k = _rope(k, cos, slo, shi)
    q_ref[...] = (q * q_scale).astype(BF16)
    k_ref[...] = k.astype(BF16)
    v_ref[...] = v.astype(BF16)


def _inproj(x, mod, g, w_in, bd, *, seq, mod_row0, rope_tables, cache, q_scale):
    n, d = x.shape
    tm = min(TOKEN_TILE, seq)
    fw = bd.shape[0]
    dw = (w_in.shape[1] - fw) // 3
    per_seq = seq // tm
    row = lambda i: (i, 0)
    in_specs = [pl.BlockSpec((tm, d), row),
                pl.BlockSpec((1, 6, d), lambda i: (mod_row0 + (i // per_seq if mod_row0 else 0), 0, 0)),
                _const_spec((1, d)), _const_spec(w_in.shape), _const_spec(bd.shape)]
    args = [x, mod, g, w_in, bd]
    if rope_tables is not None:
        in_specs += [pl.BlockSpec((tm, LANES), lambda i: (i % per_seq, 0))] * 3
        args += list(rope_tables)
    out_shape = [jax.ShapeDtypeStruct((n, 2 * fw), BF16)] + [jax.ShapeDtypeStruct((n, dw), BF16)] * 3
    out_specs = [pl.BlockSpec((tm, 2 * fw), row)] + [pl.BlockSpec((tm, dw), row)] * 3
    if cache:
        out_shape += [jax.ShapeDtypeStruct((n, dw), F32)] * 2
        out_specs += [pl.BlockSpec((tm, dw), row)] * 2
    return pl.pallas_call(
        functools.partial(_inproj_kernel, rope=rope_tables is not None, cache=cache, q_scale=q_scale),
        out_shape=out_shape, grid=(n // tm,), in_specs=in_specs, out_specs=out_specs,
        compiler_params=_cparams(("parallel",)),
        name="inproj_ctx" if cache else "inproj_lat",
    )(*args)


def _fourier_short_kernel(ct_ref, st_ref, acas_ref, o_ref, *, scale):
    fw = o_ref.shape[1]
    y = _dot(ct_ref[...], acas_ref[:, 0:fw]) - _dot(st_ref[...], acas_ref[:, fw:2 * fw])
    o_ref[...] = (y * scale).astype(BF16)


def _fourier_short(acas, seq):
    n, fw2 = acas.shape
    fw = fw2 // 2
    idx = (np.arange(seq)[:, None] * np.arange(seq)[None, :]) % seq
    ang = (2.0 * np.pi / seq) * idx
    ct = jnp.asarray(np.cos(ang), BF16)
    st = jnp.asarray(np.sin(ang), BF16)
    scale = 1.0 / math.sqrt(seq * FOURIER_GROUP_DIM)
    return pl.pallas_call(
        functools.partial(_fourier_short_kernel, scale=scale),
        out_shape=jax.ShapeDtypeStruct((n, fw), BF16),
        grid=(n // seq,),
        in_specs=[_const_spec((seq, seq)), _const_spec((seq, seq)),
                  pl.BlockSpec((seq, fw2), lambda b: (b, 0))],
        out_specs=pl.BlockSpec((seq, fw), lambda b: (b, 0)),
        compiler_params=_cparams(("parallel",)),
        name="fourier_ctx",
    )(ct, st, acas)


def _fourier_long_kernel(cb_ref, sb_ref, ca_ref, sa_ref, acas_ref, o_ref, ct_scr, st_scr, *, scale):
    @pl.when(pl.program_id(1) == 0)
    def _():
        tb, seq = cb_ref.shape
        cw = 512
        for part in range(ca_ref.shape[0]):
            rows = slice(part * tb, (part + 1) * tb)
            for c0 in range(0, seq, cw):
                ca, sa = ca_ref[part, :, c0:c0 + cw], sa_ref[part, :, c0:c0 + cw]
                cb, sb = cb_ref[:, c0:c0 + cw], sb_ref[:, c0:c0 + cw]
                ct_scr[rows, c0:c0 + cw] = (ca * cb - sa * sb).astype(BF16)
                st_scr[rows, c0:c0 + cw] = (sa * cb + ca * sb).astype(BF16)
    fw = o_ref.shape[1]
    y = _dot(ct_scr[...], acas_ref[:, 0:fw]) - _dot(st_scr[...], acas_ref[:, fw:2 * fw])
    o_ref[...] = (y * scale).astype(BF16)


def _fourier_long(acas, seq):
    n, fw2 = acas.shape
    fw = fw2 // 2
    nb = n // seq
    tr = min(FOURIER_ROW_TILE, seq)
    tb = min(FOURIER_TABLE_ROWS, tr)
    parts = tr // tb
    nr = seq // tr
    col = np.arange(seq)[None, :]
    beta = (2.0 * np.pi / seq) * ((np.arange(tb)[:, None] * col) % seq)
    alpha = (2.0 * np.pi / seq) * (((np.arange(seq // tb) * tb)[:, None] * col) % seq)
    cb, sb = jnp.asarray(np.cos(beta), F32), jnp.asarray(np.sin(beta), F32)
    ca = jnp.asarray(np.cos(alpha), F32).reshape(seq // tb, 1, seq)
    sa = jnp.asarray(np.sin(alpha), F32).reshape(seq // tb, 1, seq)
    scale = 1.0 / math.sqrt(seq * FOURIER_GROUP_DIM)
    return pl.pallas_call(
        functools.partial(_fourier_long_kernel, scale=scale),
        out_shape=jax.ShapeDtypeStruct((n, fw), BF16),
        grid=(nr, nb),
        in_specs=[_const_spec((tb, seq)), _const_spec((tb, seq)),
                  pl.BlockSpec((parts, 1, seq), lambda r, b: (r, 0, 0)),
                  pl.BlockSpec((parts, 1, seq), lambda r, b: (r, 0, 0)),
                  pl.BlockSpec((seq, fw2), lambda r, b: (b, 0))],
        out_specs=pl.BlockSpec((tr, fw), lambda r, b: (b * nr + r, 0)),
        scratch_shapes=[pltpu.VMEM((tr, seq), BF16), pltpu.VMEM((tr, seq), BF16)],
        compiler_params=_cparams(("arbitrary", "arbitrary")),
        name="fourier_lat",
    )(cb, sb, ca, sa, acas)


def _col_reduce(x, op):
    r = x.reshape(x.shape[0] // SUBLANES, SUBLANES, x.shape[1])
    n = r.shape[0]
    while n > 1:
        n //= 2
        r = op(r[:n], r[n:2 * n])
    r = r[0]
    for shift in (4, 2, 1):
        r = op(r, pltpu.roll(r, shift, 0))
    return r[0:1]


def _attn_kernel(*refs, n_chunks, tk, has_ctx, lam_init):
    lam_ref, sub_ref, q_ref, k_ref, v_ref = refs[:5]
    pos = 5
    if has_ctx:
        kc_ref, vc_ref = refs[5:7]
        pos = 7
    o_ref, vt_scr, acc_scr, s_scr = refs[pos:pos + 4]
    if has_ctx:
        vtc_scr = refs[pos + 4]
    tq, hd2 = q_ref.shape
    half = hd2 // 2
    depth = s_scr.shape[0]

    @pl.when(pl.program_id(2) == 0)
    def _():
        for c in range(n_chunks):
            vt_scr[c] = v_ref[c * tk:(c + 1) * tk, :].astype(F32).T.astype(BF16)
        if has_ctx:
            vtc_scr[...] = vc_ref[...].astype(F32).T.astype(BF16)

    qt = q_ref[...].astype(F32).T
    row = lax.broadcasted_iota(jnp.int32, qt.shape, 0)
    qts = (jnp.where(row < half, qt, 0.0).astype(BF16),
           jnp.where(row >= half, qt, 0.0).astype(BF16))

    chunks = [(k_ref, c * tk, tk, functools.partial(vt_scr.__getitem__, c)) for c in range(n_chunks)]
    if has_ctx:
        chunks.append((kc_ref, 0, kc_ref.shape[0], functools.partial(vtc_scr.__getitem__, Ellipsis)))

    def scores(idx):
        k_src, start, size, _ = chunks[idx]
        kc = k_src[start:start + size, :]
        for j in range(2):
            s_scr[idx % depth, j, 0:size, :] = _dot(kc, qts[j])

    def softmax_pv(idx, state):
        _, _, size, vt = chunks[idx]
        new = []
        for j in range(2):
            s = s_scr[idx % depth, j, 0:size, :]
            m_blk = _col_reduce(s, jnp.maximum)
            if state is None:
                m_new = m_blk
                p = jnp.exp2(s - m_new)
                l_new = _col_reduce(p, jnp.add)
                acc_scr[j] = _dot(vt(), p.astype(BF16))
            else:
                m, l = state[j]
                m_new = jnp.maximum(m, m_blk)
                alpha = jnp.exp2(m - m_new)
                p = jnp.exp2(s - m_new)
                l_new = alpha * l + _col_reduce(p, jnp.add)
                acc_scr[j] = alpha * acc_scr[j] + _dot(vt(), p.astype(BF16))
            new.append((m_new, l_new))
        return tuple(new)

    state = None
    ahead = depth - 1
    for idx in range(min(ahead, len(chunks))):
        scores(idx)
    for idx in range(len(chunks)):
        if idx + ahead < len(chunks):
            scores(idx + ahead)
        state = softmax_pv(idx, state)
    (_, l1), (_, l2) = state
    lp = lam_ref[...]
    lam = (jnp.exp(jnp.sum(lp[0:1] * lp[1:2], axis=-1, keepdims=True))
           - jnp.exp(jnp.sum(lp[2:3] * lp[3:4], axis=-1, keepdims=True)) + lam_init)
    ot = acc_scr[0] * (1.0 / l1) - lam * (acc_scr[1] * (1.0 / l2))
    o = ot.T
    o = o * lax.rsqrt(jnp.mean(o * o, axis=-1, keepdims=True) + SUBLN_EPS)
    o_ref[...] = (o * sub_ref[...] * (1.0 - lam_init)).astype(BF16)


def _attention(lam_p, sub_g, q, k, v, k_ctx, v_ctx, *, seq, lam_init):
    n, dw = q.shape
    nb = n // seq
    hd2 = dw // N_HEADS
    tq = min(ATTN_Q_TILE, seq)
    tk = min(ATTN_K_TILE, seq)
    nq = seq // tq
    has_ctx = k_ctx is not None
    in_specs = [_const_spec(lam_p.shape), _const_spec(sub_g.shape),
                pl.BlockSpec((tq, hd2), lambda b, h, i: (b * nq + i, h)),
                pl.BlockSpec((seq, hd2), lambda b, h, i: (b, h)),
                pl.BlockSpec((seq, hd2), lambda b, h, i: (b, h))]
    args = [lam_p, sub_g, q, k, v]
    scratch = [pltpu.VMEM((seq // tk, hd2, tk), BF16), pltpu.VMEM((2, hd2, tq), F32),
               pltpu.VMEM((ATTN_SCORE_BUFFERS, 2, tk, tq), F32)]
    if has_ctx:
        lc = k_ctx.shape[0] // nb
        in_specs += [pl.BlockSpec((lc, hd2), lambda b, h, i: (b, h))] * 2
        args += [k_ctx, v_ctx]
        scratch.append(pltpu.VMEM((hd2, lc), BF16))
    return pl.pallas_call(
        functools.partial(_attn_kernel, n_chunks=seq // tk, tk=tk, has_ctx=has_ctx, lam_init=lam_init),
        out_shape=jax.ShapeDtypeStruct((n, dw), BF16),
        grid=(nb, N_HEADS, nq),
        in_specs=in_specs,
        out_specs=pl.BlockSpec((tq, hd2), lambda b, h, i: (b * nq + i, h)),
        scratch_shapes=scratch,
        compiler_params=_cparams(("parallel", "parallel", "arbitrary")),
        name="attn_lat" if has_ctx else "attn_ctx",
    )(*args)


def _outproj_kernel(xc_ref, fc_ref, ac_ref, xl_ref, fl_ref, al_ref, w_ref, mod_ref, o_ref, *, ctx_tiles):
    def residual(x_ref, four_ref, attn_ref):
        fw = four_ref.shape[1]
        y = _dot(four_ref[...], w_ref[0:fw, :]) + _dot(attn_ref[...], w_ref[fw:, :])
        o_ref[...] = x_ref[...] + mod_ref[0][2:3] * y

    @pl.when(pl.program_id(0) < ctx_tiles)
    def _():
        residual(xc_ref, fc_ref, ac_ref)

    @pl.when(pl.program_id(0) >= ctx_tiles)
    def _():
        residual(xl_ref, fl_ref, al_ref)


def _outproj(ctx, lat, w_out, mod, *, seq):
    n_ctx, d = ctx[0].shape
    n_lat = lat[0].shape[0]
    tm = TOKEN_TILE
    tc = n_ctx // tm
    ctx_row = lambda i: (jnp.minimum(i, tc - 1), 0)
    lat_row = lambda i: (jnp.maximum(i - tc, 0), 0)
    in_specs = ([pl.BlockSpec((tm, a.shape[1]), ctx_row) for a in ctx]
                + [pl.BlockSpec((tm, a.shape[1]), lat_row) for a in lat]
                + [_const_spec(w_out.shape), pl.BlockSpec((1, 6, d), _seg_index_map(tm, n_ctx, seq))])
    return pl.pallas_call(
        functools.partial(_outproj_kernel, ctx_tiles=tc),
        out_shape=jax.ShapeDtypeStruct((n_ctx + n_lat, d), F32),
        grid=((n_ctx + n_lat) // tm,), in_specs=in_specs,
        out_specs=pl.BlockSpec((tm, d), lambda i: (i, 0)),
        compiler_params=_cparams(("parallel",)),
        name="outproj",
    )(*ctx, *lat, w_out, mod)


def _ffn_kernel(x_ref, mod_ref, g_ref, w1_ref, w3_ref, w2_ref, o_ref):
    m = mod_ref[0]
    x = x_ref[...]
    h = _norm_mod(x, g_ref[...], m[3:4], m[4:5]).astype(BF16)
    a = (_silu(_dot(h, w1_ref[...])) * _dot(h, w3_ref[...])).astype(BF16)
    o_ref[...] = x + m[5:6] * _dot(a, w2_ref[...])


def _ffn(x_all, mod, g, w1, w3, w2, *, n_ctx, seq):
    n, d = x_all.shape
    tm = TOKEN_TILE
    row = lambda i: (i, 0)
    return pl.pallas_call(
        _ffn_kernel,
        out_shape=jax.ShapeDtypeStruct((n, d), F32),
        grid=(n // tm,),
        in_specs=[pl.BlockSpec((tm, d), row), pl.BlockSpec((1, 6, d), _seg_index_map(tm, n_ctx, seq)),
                  _const_spec((1, d)), _const_spec(w1.shape), _const_spec(w3.shape), _const_spec(w2.shape)],
        out_specs=pl.BlockSpec((tm, d), row),
        input_output_aliases={0: 0},
        compiler_params=_cparams(("parallel",)),
        name="ffn_dense",
    )(x_all, mod, g, w1, w3, w2)


def _pw1_kernel(x_ref, mod_ref, g_ref, w_ref, u_ref):
    m = mod_ref[0]
    h = _norm_mod(x_ref[...], g_ref[...], m[0:1], m[1:2]).astype(BF16)
    d = u_ref.shape[1]
    u_ref[...] = _dot(h, w_ref[:, 0:d]) * jax.nn.sigmoid(_dot(h, w_ref[:, d:2 * d]))


def _pw1(x_all, mod, g, w_pw1, *, n_ctx, seq):
    n, d = x_all.shape
    tm = TOKEN_TILE
    row = lambda i: (i, 0)
    return pl.pallas_call(
        _pw1_kernel,
        out_shape=jax.ShapeDtypeStruct((n, d), F32),
        grid=(n // tm,),
        in_specs=[pl.BlockSpec((tm, d), row), pl.BlockSpec((1, 6, d), _seg_index_map(tm, n_ctx, seq)),
                  _const_spec((1, d)), _const_spec(w_pw1.shape)],
        out_specs=pl.BlockSpec((tm, d), row),
        compiler_params=_cparams(("parallel",)),
        name="conv_pw1",
    )(x_all, mod, g, w_pw1)


def _conv_kernel(u_ref, prev_ref, next_ref, wdw_ref, bdw_ref, lng_ref, lnb_ref, w2_ref, x_ref, mod_ref,
                 o_ref, win_scr, conv_scr, shift_scr, *, n_ctx, seq_ctx, seq_lat, taps, front):
    tm, d = u_ref.shape
    start = pl.program_id(0) * tm
    is_ctx = start < n_ctx
    off = jnp.where(is_ctx, start % seq_ctx, (start - n_ctx) % seq_lat)
    slen = jnp.where(is_ctx, seq_ctx, seq_lat)
    win_scr[0:CONV_HALO, :] = jnp.where(off == 0, 0.0, prev_ref[...])
    win_scr[CONV_HALO:CONV_HALO + tm, :] = u_ref[...]
    win_scr[CONV_HALO + tm:, :] = jnp.where(off + tm == slen, 0.0, next_ref[...])
    n_q = wdw_ref.shape[0]
    rc = CONV_ROW_CHUNK
    for cb in range(d // LANES):
        cols = slice(cb * LANES, (cb + 1) * LANES)
        for k in range(1, SUBLANES):
            shift_scr[k] = win_scr[k:k + tm + n_q - SUBLANES, cols]
        w = {q: wdw_ref[q, :, cols] for q in range(front, front + taps)}
        bias = bdw_ref[:, cols]
        for r0 in range(0, tm, rc):
            acc = None
            for q in range(front, front + taps):
                k = q % SUBLANES
                if k:
                    tap = shift_scr[k, r0 + q - k:r0 + q - k + rc, :]
                else:
                    tap = win_scr[r0 + q:r0 + q + rc, cols]
                term = tap.reshape(rc // SUBLANES, SUBLANES, LANES) * w[q][None]
                acc = term if acc is None else acc + term
            conv_scr[r0:r0 + rc, cols] = acc.reshape(rc, LANES) + bias
    c = conv_scr[...]
    mu = jnp.mean(c, axis=-1, keepdims=True)
    cc = c - mu
    var = jnp.mean(cc * cc, axis=-1, keepdims=True)
    y = _silu(cc * lax.rsqrt(var + NORM_EPS) * lng_ref[...] + lnb_ref[...]).astype(BF16)
    o_ref[...] = x_ref[...] + mod_ref[0][2:3] * _dot(y, w2_ref[...])


def _conv(u, x_all, mod, w_dw, b_dw, ln_g, ln_b, w_pw2, *, n_ctx, seq_ctx, seq_lat):
    n, d = u.shape
    tm = min(CONV_TILE, seq_ctx)
    taps = w_dw.shape[0]
    front = CONV_HALO - taps // 2
    w_dw = jnp.pad(w_dw, ((front, 2 * CONV_HALO - taps - front), (0, 0)))
    w_dw = jnp.broadcast_to(w_dw[:, None, :], (2 * CONV_HALO, SUBLANES, d))
    hb = tm // CONV_HALO
    last = n // CONV_HALO - 1
    row = lambda i: (i, 0)
    return pl.pallas_call(
        functools.partial(_conv_kernel, n_ctx=n_ctx, seq_ctx=seq_ctx, seq_lat=seq_lat, taps=taps, front=front),
        out_shape=jax.ShapeDtypeStruct((n, d), F32),
        grid=(n // tm,),
        in_specs=[pl.BlockSpec((tm, d), row),
                  pl.BlockSpec((CONV_HALO, d), lambda i: (jnp.maximum(i * hb - 1, 0), 0)),
                  pl.BlockSpec((CONV_HALO, d), lambda i: (jnp.minimum((i + 1) * hb, last), 0)),
                  _const_spec(w_dw.shape), _const_spec((1, d)), _const_spec((1, d)), _const_spec((1, d)),
                  _const_spec(w_pw2.shape),
                  pl.BlockSpec((tm, d), row),
                  pl.BlockSpec((1, 6, d), _seg_index_map(tm, n_ctx, seq_lat))],
        out_specs=pl.BlockSpec((tm, d), row),
        scratch_shapes=[pltpu.VMEM((tm + 2 * CONV_HALO, d), F32), pltpu.VMEM((tm, d), F32),
                        pltpu.VMEM((SUBLANES, tm + 2 * CONV_HALO - SUBLANES, LANES), F32)],
        input_output_aliases={8: 0},
        compiler_params=_cparams(("parallel",)),
        name="conv_dw_pw2",
    )(u, u, u, w_dw, b_dw, ln_g, ln_b, w_pw2, x_all, mod)


def _router_kernel(x_ref, mod_ref, g_ref, wr_ref, tri_ref, upper_ref, h_ref, info_ref, runs_ref, *, n_experts):
    m = mod_ref[0]
    h = _norm_mod(x_ref[...], g_ref[...], m[3:4], m[4:5])
    h_hi = h.astype(BF16)
    h_ref[...] = h_hi
    h_lo = (h - h_hi.astype(F32)).astype(BF16)
    wr = wr_ref[...]
    w_hi = wr.astype(BF16)
    w_lo = (wr - w_hi.astype(F32)).astype(BF16)
    logits = _dot(h_hi, w_hi) + (_dot(h_lo, w_hi) + _dot(h_hi, w_lo))
    lane = lax.broadcasted_iota(jnp.int32, logits.shape, 1)
    neg = jnp.float32(-jnp.inf)
    logits = jnp.where(lane < n_experts, logits, neg)
    m1 = jnp.max(logits, axis=-1, keepdims=True)
    i1 = jnp.min(jnp.where(logits == m1, lane, LANES), axis=-1, keepdims=True)
    oh1 = lane == i1
    rest = jnp.where(oh1, neg, logits)
    m2 = jnp.max(rest, axis=-1, keepdims=True)
    i2 = jnp.min(jnp.where(rest == m2, lane, LANES), axis=-1, keepdims=True)
    oh2 = lane == i2
    e = jnp.exp(m2 - m1)
    g1 = 1.0 / (1.0 + e)
    g2 = e * g1
    oh = jnp.where(oh1, 1.0, 0.0) + jnp.where(oh2, 1.0, 0.0)
    cnt = jnp.sum(oh, axis=0, keepdims=True)
    run_units = jnp.floor((cnt + (RUN_ALIGN - 1.0)) * (1.0 / RUN_ALIGN))
    units8 = jnp.broadcast_to(run_units, (SUBLANES, LANES)).astype(BF16)
    run_start = _dot(units8, upper_ref[...])[0:1, :] * RUN_ALIGN
    before = _dot(tri_ref[...], oh.astype(BF16)) + run_start
    p1 = jnp.sum(jnp.where(oh1, before, 0.0), axis=-1, keepdims=True)
    p2 = jnp.sum(jnp.where(oh2, before, 0.0), axis=-1, keepdims=True)
    info = jnp.where(lane == 0, p1,
           jnp.where(lane == 1, p2,
           jnp.where(lane == 2, g1,
           jnp.where(lane == 3, g2, 0.0))))
    info_ref[...] = info
    sub = lax.broadcasted_iota(jnp.int32, (SUBLANES, LANES), 0)
    runs_ref[0] = jnp.where(sub == 0, run_units * RUN_ALIGN, jnp.where(sub == 1, run_start, 0.0))


def _router(x_all, mod, g, w_router, *, n_ctx, seq):
    n, d = x_all.shape
    n_experts = w_router.shape[1]
    tm = TOKEN_TILE
    wr = jnp.pad(w_router, ((0, 0), (0, LANES - n_experts)))
    tri = jnp.asarray(np.tril(np.ones((tm, tm), np.float32), -1), BF16)
    upper = jnp.asarray(np.triu(np.ones((LANES, LANES), np.float32), 1), BF16)
    row = lambda i: (i, 0)
    return pl.pallas_call(
        functools.partial(_router_kernel, n_experts=n_experts),
        out_shape=[jax.ShapeDtypeStruct((n, d), BF16), jax.ShapeDtypeStruct((n, LANES), F32),
                   jax.ShapeDtypeStruct((n // tm, SUBLANES, LANES), F32)],
        grid=(n // tm,),
        in_specs=[pl.BlockSpec((tm, d), row), pl.BlockSpec((1, 6, d), _seg_index_map(tm, n_ctx, seq)),
                  _const_spec((1, d)), _const_spec(wr.shape), _const_spec(tri.shape), _const_spec(upper.shape)],
        out_specs=[pl.BlockSpec((tm, d), row), pl.BlockSpec((tm, LANES), row),
                   pl.BlockSpec((1, SUBLANES, LANES), lambda i: (i, 0, 0))],
        compiler_params=_cparams(("parallel",)),
        name="moe_router",
    )(x_all, mod, g, wr, tri, upper)


def _run_copies(n_experts, len_ref, start_ref, dst_ref, tile, make_copy):
    ops = []
    for e in range(n_experts):
        n = len_ref[tile * n_experts + e]
        src = start_ref[tile * n_experts + e]
        dst = dst_ref[tile * n_experts + e]
        ops.append((n > 0, make_copy(pl.multiple_of(src, RUN_ALIGN), pl.multiple_of(dst, RUN_ALIGN),
                                     pl.multiple_of(n, RUN_ALIGN))))
    return ops


def _dispatch_kernel(zrow_ref, len_ref, start_ref, dst_ref, h_ref, info_ref, xg_ref, zero_scr, xs_scr, sem, zsem,
                     *, n_zero, n_experts):
    tm = h_ref.shape[0]
    tz = zero_scr.shape[0]
    rows = xs_scr.shape[1]

    def zero_copy(e):
        return pltpu.make_async_copy(zero_scr, xg_ref.at[pl.ds(pl.multiple_of(zrow_ref[e], tz), tz)], zsem)

    @pl.when(pl.program_id(0) == 0)
    def _():
        zero_scr[...] = jnp.zeros_like(zero_scr)
        for e in range(n_zero):
            @pl.when(zrow_ref[e] >= 0)
            def _():
                zero_copy(e).start()
        for e in range(n_zero):
            @pl.when(zrow_ref[e] >= 0)
            def _():
                zero_copy(e).wait()

    info_t = info_ref[...].T
    slot1 = info_t[0:1, :].astype(jnp.int32)
    slot2 = info_t[1:2, :].astype(jnp.int32)
    r = lax.broadcasted_iota(jnp.int32, (rows, tm), 0)
    perm = (jnp.where(r == slot1, 1.0, 0.0) + jnp.where(r == slot2, 1.0, 0.0)).astype(BF16)
    step = pl.program_id(0)
    buf = step % 2
    xs_scr[buf] = _dot(perm, h_ref[...])

    def run_writes(tile, b):
        def make_copy(src, dst, size):
            return pltpu.make_async_copy(xs_scr.at[b, pl.ds(src, size)], xg_ref.at[pl.ds(dst, size)], sem.at[b])
        return _run_copies(n_experts, len_ref, start_ref, dst_ref, tile, make_copy)

    for cond, copy in run_writes(step, buf):
        pl.when(cond)(copy.start)

    @pl.when(step > 0)
    def _():
        for cond, copy in run_writes(step - 1, 1 - buf):
            pl.when(cond)(copy.wait)

    @pl.when(step == pl.num_programs(0) - 1)
    def _():
        for cond, copy in run_writes(step, buf):
            pl.when(cond)(copy.wait)


def _dispatch(h_all, info, run_len, run_start, run_dst, zero_tile_row, n_rows):
    n, d = h_all.shape
    tm = TOKEN_TILE
    n_zero = zero_tile_row.shape[0]
    n_experts = run_len.shape[0] // (n // tm)
    grid_spec = pltpu.PrefetchScalarGridSpec(
        num_scalar_prefetch=4,
        grid=(n // tm,),
        in_specs=[pl.BlockSpec((tm, d), lambda i, *_: (i, 0)),
                  pl.BlockSpec((tm, LANES), lambda i, *_: (i, 0))],
        out_specs=pl.BlockSpec(memory_space=pl.ANY),
        scratch_shapes=[pltpu.VMEM((MOE_TILE, d), F32), pltpu.VMEM((2, _local_rows(tm, n_experts), d), F32),
                        pltpu.SemaphoreType.DMA((2,)), pltpu.SemaphoreType.DMA(())],
    )
    return pl.pallas_call(
        functools.partial(_dispatch_kernel, n_zero=n_zero, n_experts=n_experts),
        out_shape=jax.ShapeDtypeStruct((n_rows, d), F32),
        grid_spec=grid_spec,
        compiler_params=_cparams(("arbitrary",)),
        name="moe_dispatch",
    )(zero_tile_row, run_len, run_start, run_dst, h_all, info)


def _experts_kernel(te_ref, nt_ref, x_ref, w1_ref, w3_ref, w2_ref, o_ref, xb_scr):
    c = pl.program_id(1)

    @pl.when(pl.program_id(0) < nt_ref[0])
    def _():
        @pl.when(c == 0)
        def _():
            xb_scr[...] = x_ref[...].astype(BF16)
        xb = xb_scr[...]
        a = (_silu(_dot(xb, w1_ref[0])) * _dot(xb, w3_ref[0])).astype(BF16)
        y = _dot(a, w2_ref[0])

        @pl.when(c == 0)
        def _():
            o_ref[...] = y

        @pl.when(c > 0)
        def _():
            o_ref[...] += y

    @pl.when(jnp.logical_and(pl.program_id(0) >= nt_ref[0], c == 0))
    def _():
        o_ref[...] = jnp.zeros_like(o_ref)


def _experts(xg, w1, w3, w2, tile_expert, n_tiles_used):
    p, d = xg.shape
    tm = MOE_TILE
    ff = w1.shape[2]
    fc = MOE_FF_CHUNK if ff % MOE_FF_CHUNK == 0 else ff
    nc = ff // fc
    n_tiles = p // tm

    def tile(t, nt):
        return jnp.minimum(t, nt[0] - 1)

    def chunk(t, c, nt):
        return jnp.where(t < nt[0], c, nc - 1)

    grid_spec = pltpu.PrefetchScalarGridSpec(
        num_scalar_prefetch=2,
        grid=(n_tiles, nc),
        in_specs=[pl.BlockSpec((tm, d), lambda t, c, te, nt: (tile(t, nt), 0)),
                  pl.BlockSpec((1, d, fc), lambda t, c, te, nt: (te[tile(t, nt)], 0, chunk(t, c, nt))),
                  pl.BlockSpec((1, d, fc), lambda t, c, te, nt: (te[tile(t, nt)], 0, chunk(t, c, nt))),
                  pl.BlockSpec((1, fc, d), lambda t, c, te, nt: (te[tile(t, nt)], chunk(t, c, nt), 0))],
        out_specs=pl.BlockSpec((tm, d), lambda t, c, te, nt: (t, 0)),
        scratch_shapes=[pltpu.VMEM((tm, d), BF16)],
    )
    return pl.pallas_call(
        _experts_kernel,
        out_shape=jax.ShapeDtypeStruct((p, d), F32),
        grid_spec=grid_spec,
        compiler_params=_cparams(("arbitrary", "arbitrary")),
        name="moe_experts",
    )(tile_expert, n_tiles_used, xg, w1, w3, w2)


def _combine_kernel(len_ref, start_ref, dst_ref, yg_ref, x_ref, info_ref, mod_ref, fg_ref, o_ref, ys_scr, sem,
                    *, n_experts, tile0):
    tm = x_ref.shape[0]
    rows = ys_scr.shape[1]
    step = pl.program_id(0)
    buf = step % 2

    def run_reads(tile, b):
        def make_copy(src, dst, size):
            return pltpu.make_async_copy(yg_ref.at[pl.ds(dst, size)], ys_scr.at[b, pl.ds(src, size)], sem.at[b])
        return _run_copies(n_experts, len_ref, start_ref, dst_ref, tile0 + tile, make_copy)

    @pl.when(step == 0)
    def _():
        ys_scr[...] = jnp.zeros_like(ys_scr)
        for cond, copy in run_reads(step, buf):
            pl.when(cond)(copy.start)

    @pl.when(step + 1 < pl.num_programs(0))
    def _():
        for cond, copy in run_reads(step + 1, 1 - buf):
            pl.when(cond)(copy.start)

    for cond, copy in run_reads(step, buf):
        pl.when(cond)(copy.wait)
    info = info_ref[...]
    ys = ys_scr[buf].astype(BF16)
    lane = lax.broadcasted_iota(jnp.int32, (tm, rows), 1)
    pick1 = jnp.where(lane == info[:, 0:1].astype(jnp.int32), 1.0, 0.0).astype(BF16)
    pick2 = jnp.where(lane == info[:, 1:2].astype(jnp.int32), 1.0, 0.0).astype(BF16)
    y = info[:, 2:3] * _dot(pick1, ys) + info[:, 3:4] * _dot(pick2, ys)
    x = x_ref[...] + mod_ref[0][5:6] * y
    o_ref[...] = x * lax.rsqrt(jnp.mean(x * x, axis=-1, keepdims=True) + NORM_EPS) * fg_ref[...]


def _combine(yg, x_all, info, run_len, run_start, run_dst, mod, final_g, *, row0, n_rows, seq, mod_row0):
    n, d = x_all.shape
    tm = TOKEN_TILE
    t0 = row0 // tm
    per_seq = seq // tm
    n_experts = run_len.shape[0] // (n // tm)
    grid_spec = pltpu.PrefetchScalarGridSpec(
        num_scalar_prefetch=3,
        grid=(n_rows // tm,),
        in_specs=[pl.BlockSpec(memory_space=pl.ANY),
                  pl.BlockSpec((tm, d), lambda i, *_: (t0 + i, 0)),
                  pl.BlockSpec((tm, LANES), lambda i, *_: (t0 + i, 0)),
                  pl.BlockSpec((1, 6, d), lambda i, *_: (mod_row0 + (i // per_seq if mod_row0 else 0), 0, 0)),
                  pl.BlockSpec((1, d), lambda i, *_: (0, 0))],
        out_specs=pl.BlockSpec((tm, d), lambda i, *_: (i, 0)),
        scratch_shapes=[pltpu.VMEM((2, _local_rows(tm, n_experts), d), F32), pltpu.SemaphoreType.DMA((2,))],
    )
    return pl.pallas_call(
        functools.partial(_combine_kernel, n_experts=n_experts, tile0=t0),
        out_shape=jax.ShapeDtypeStruct((n_rows, d), F32),
        grid_spec=grid_spec,
        compiler_params=_cparams(("arbitrary",)),
        name="moe_combine",
    )(run_len, run_start, run_dst, yg, x_all, info, mod, final_g)


def _moe(x_all, mod, g, w_router, w1, w3, w2, final_g, *, n_ctx, seq_ctx_total, seq_lat):
    n, d = x_all.shape
    n_experts = w_router.shape[1]
    tm = TOKEN_TILE
    h_all, info, runs = _router(x_all, mod, g, w_router, n_ctx=n_ctx, seq=seq_lat)
    run_len = runs[:, 0, :n_experts].astype(jnp.int32)
    run_start = runs[:, 1, :n_experts].astype(jnp.int32)
    cnt = jnp.sum(run_len, axis=0)
    padded = ((cnt + MOE_TILE - 1) // MOE_TILE) * MOE_TILE
    ends = jnp.cumsum(padded)
    offs = ends - padded
    run_dst = offs[None, :] + jnp.cumsum(run_len, axis=0) - run_len
    n_tiles = -(-(TOP_K * n + (n // tm) * n_experts * (RUN_ALIGN - 1)) // MOE_TILE) + n_experts
    n_used = (ends[-1] // MOE_TILE).astype(jnp.int32)
    tile_start = jnp.arange(n_tiles, dtype=jnp.int32) * MOE_TILE
    tile_expert = jnp.minimum(jnp.sum(tile_start[:, None] >= ends[None, :], axis=1), n_experts - 1).astype(jnp.int32)
    tile_expert = jnp.where(tile_start < ends[-1], tile_expert, tile_expert[jnp.maximum(n_used - 1, 0)])
    tail_start = tile_start[(TOP_K * n) // MOE_TILE:]
    zero_tile_row = jnp.concatenate([jnp.where(padded > 0, ends - MOE_TILE, -1),
                                     jnp.where(tail_start >= ends[-1], tail_start, -1)]).astype(jnp.int32)
    run_args = (run_len.reshape(-1), run_start.reshape(-1), run_dst.reshape(-1).astype(jnp.int32))
    xg = _dispatch(h_all, info, *run_args, zero_tile_row, n_tiles * MOE_TILE)
    yg = _experts(xg, w1, w3, w2, tile_expert, n_used.reshape(1))
    y_ctx = _combine(yg, x_all, info, *run_args, mod, final_g, row0=0, n_rows=n_ctx, seq=seq_ctx_total, mod_row0=0)
    y_lat = _combine(yg, x_all, info, *run_args, mod, final_g, row0=n_ctx, n_rows=n - n_ctx, seq=seq_lat,
                     mod_row0=1)
    return y_ctx, y_lat


def _rope_tables(seq, hd):
    axis_dim = hd // 2
    nf = axis_dim // 2
    inv = jnp.power(ROPE_BASE, -jnp.arange(nf, dtype=F32) / nf)
    pos = np.arange(seq)
    row = jnp.asarray(pos // GRID_W, F32)
    col = jnp.asarray(pos % GRID_W, F32)
    lane = np.arange(2 * hd)
    dd = lane % hd
    use_col = (dd // axis_dim) == 1
    e = dd % axis_dim
    first = jnp.asarray(e < nf)[None, :]
    freq = inv[e % nf]
    ang = jnp.where(jnp.asarray(use_col)[None, :], col[:, None], row[:, None]) * freq[None, :]
    cos, sin = jnp.cos(ang), jnp.sin(ang)
    return cos, jnp.where(first, -sin, 0.0), jnp.where(first, 0.0, sin)


def _channel_dft(fw):
    g = FOURIER_GROUP_DIM
    ang = (2.0 * np.pi / g) * ((np.arange(g)[:, None] * np.arange(g)[None, :]) % g)
    eye = np.eye(fw // g)
    return jnp.asarray(np.concatenate([np.kron(eye, np.cos(ang)), np.kron(eye, np.sin(ang))], axis=1), BF16)


def kernel(x_prompt, x_sample, cache_k, cache_v, c, c_ctx, w_ada, b_ada, norm_g, final_g, w_in_ab, w_out_ab,
           diff_lambda, diff_subln, conv_w_pw1, conv_w_dw, conv_b_dw, conv_ln_g, conv_ln_b, conv_w_pw2,
           ffn_w1, ffn_w3, ffn_w2, moe_router, moe_w1, moe_w3, moe_w2):
    bp, lp, d = x_prompt.shape
    bs, ls, _ = x_sample.shape
    lc = cache_k.shape[2]
    n_ctx, n_lat = bp * lp, bs * ls
    n_all = n_ctx + n_lat
    dw = N_HEADS * cache_k.shape[-1]
    hd = cache_k.shape[-1] // 2
    fw = w_in_ab.shape[2] - 3 * dw

    rows = -(-(1 + bs) // SUBLANES) * SUBLANES
    cond = jnp.zeros((rows, d), F32).at[0].set(c_ctx).at[1:1 + bs].set(c)
    mod = _modulation(cond, w_ada, b_ada)

    lam_init = 0.8 - 0.6 * math.exp(-0.3 * 0)
    q_scale = (hd ** -0.5) * LOG2E
    w_in = w_in_ab[0].astype(BF16)
    bd = _channel_dft(fw)
    g00 = norm_g[0, 0].reshape(1, d)
    xp = x_prompt.reshape(n_ctx, d)
    xs = x_sample.reshape(n_lat, d)
    acas_p, q_p, k_p, v_p, kc_new, vc_new = _inproj(
        xp, mod[0], g00, w_in, bd, seq=lp, mod_row0=0, rope_tables=None, cache=True, q_scale=q_scale)
    acas_s, q_s, k_s, v_s = _inproj(
        xs, mod[0], g00, w_in, bd, seq=ls, mod_row0=1, rope_tables=_rope_tables(ls, hd), cache=False, q_scale=q_scale)
    four_p = _fourier_short(acas_p, lp)
    four_s = _fourier_long(acas_s, ls)
    lam_p = diff_lambda[0]
    sub_g = diff_subln[0].reshape(1, 2 * hd)
    attn_p = _attention(lam_p, sub_g, q_p, k_p, v_p, None, None, seq=lp, lam_init=lam_init)
    k_ctx = cache_k[:, 0].reshape(bs * lc, dw).astype(BF16)
    v_ctx = cache_v[:, 0].reshape(bs * lc, dw).astype(BF16)
    attn_s = _attention(lam_p, sub_g, q_s, k_s, v_s, k_ctx, v_ctx, seq=ls, lam_init=lam_init)
    w_out = w_out_ab[0].astype(BF16)
    x_all = _outproj((xp, four_p, attn_p), (xs, four_s, attn_s), w_out, mod[0], seq=ls)
    x_all = _ffn(x_all, mod[0], norm_g[0, 1].reshape(1, d), ffn_w1[0].astype(BF16), ffn_w3[0].astype(BF16),
                 ffn_w2[0].astype(BF16), n_ctx=n_ctx, seq=ls)

    u = _pw1(x_all, mod[1], norm_g[1, 0].reshape(1, d), conv_w_pw1[0].astype(BF16), n_ctx=n_ctx, seq=ls)
    x_all = _conv(u, x_all, mod[1], conv_w_dw[0], conv_b_dw[0].reshape(1, d), conv_ln_g[0].reshape(1, d),
                  conv_ln_b[0].reshape(1, d), conv_w_pw2[0].astype(BF16), n_ctx=n_ctx, seq_ctx=lp, seq_lat=ls)
    y_ctx, y_lat = _moe(x_all, mod[1], norm_g[1, 1].reshape(1, d), moe_router[0], moe_w1[0].astype(BF16),
                        moe_w3[0].astype(BF16), moe_w2[0].astype(BF16), final_g.reshape(1, d),
                        n_ctx=n_ctx, seq_ctx_total=n_ctx, seq_lat=ls)

    n_even = w_in_ab.shape[0]
    new_k = kc_new.reshape(bp, n_even, lp, N_HEADS, 2 * hd)
    new_v = vc_new.reshape(bp, n_even, lp, N_HEADS, 2 * hd)
    return (y_ctx.reshape(bp, lp, d), y_lat.reshape(bs, ls, d), new_k, new_v)
```

```python
import functools
import math

import numpy as np
import jax
import jax.numpy as jnp
from jax import lax
from jax.experimental import pallas as pl
from jax.experimental.pallas import tpu as pltpu

F32 = jnp.float32
BF16 = jnp.bfloat16

GRID_W = 64
ROPE_BASE = 10000.0
N_HEADS = 4
FOURIER_GROUP_DIM = 128
CONV_HALO = 16
TOP_K = 2
NORM_EPS = 1e-6
SUBLN_EPS = 1e-5
LANES = 128
SUBLANES = 8
LOG2E = 1.4426950408889634

TOKEN_TILE = 512
CONV_TILE = 256
CONV_ROW_CHUNK = 64
ATTN_Q_TILE = 512
ATTN_K_TILE = 512
ATTN_SCORE_BUFFERS = 2
FOURIER_ROW_TILE = 512
FOURIER_TABLE_ROWS = 256
MOE_TILE = 512
MOE_FF_CHUNK = 1792
RUN_ALIGN = SUBLANES
VMEM_LIMIT = 56 * 1024 * 1024


def _local_rows(tile, n_experts):
    return -(-(TOP_K * tile + n_experts * (RUN_ALIGN - 1)) // LANES) * LANES


def _cparams(semantics, vmem=VMEM_LIMIT):
    return pltpu.CompilerParams(dimension_semantics=semantics, vmem_limit_bytes=vmem)


def _const_spec(shape):
    nd = len(shape)
    return pl.BlockSpec(shape, lambda *_: (0,) * nd, pipeline_mode=pl.Buffered(1))


def _dot(a, b):
    return jnp.dot(a, b, preferred_element_type=F32)


def _silu(x):
    return x * jax.nn.sigmoid(x)


def _norm_mod(x, g, shift, scale):
    ms = jnp.mean(x * x, axis=-1, keepdims=True)
    return (x * lax.rsqrt(ms + NORM_EPS)) * (g * (1.0 + scale)) + shift


def _seg_index_map(tile, n_ctx, seq):
    def index_map(i, *_):
        start = i * tile
        return (jnp.where(start < n_ctx, 0, 1 + (start - n_ctx) // seq), 0, 0)
    return index_map


def _mod_kernel(c_ref, w_ref, b_ref, o_ref):
    c = c_ref[...]
    o_ref[0] = _dot(_silu(c).astype(BF16), w_ref[0].astype(BF16)) + b_ref[0]


def _modulation(cond, w_ada, b_ada):
    depth, d, n6 = w_ada.shape
    rows = cond.shape[0]
    tn = 1536
    out = pl.pallas_call(
        _mod_kernel,
        out_shape=jax.ShapeDtypeStruct((depth, rows, n6), F32),
        grid=(depth, n6 // tn),
        in_specs=[pl.BlockSpec((rows, d), lambda l, j: (0, 0)),
                  pl.BlockSpec((1, d, tn), lambda l, j: (l, 0, j)),
                  pl.BlockSpec((1, 1, tn), lambda l, j: (l, 0, j))],
        out_specs=pl.BlockSpec((1, rows, tn), lambda l, j: (l, 0, j)),
        compiler_params=_cparams(("parallel", "parallel")),
        name="modulation",
    )(cond, w_ada, b_ada.reshape(depth, 1, n6))
    return out.reshape(depth, rows, 6, d)


def _rope(t, cos, sin_lo, sin_hi):
    outs = []
    for h in range(t.shape[1] // LANES):
        th = t[:, h * LANES:(h + 1) * LANES]
        outs.append(th * cos + pltpu.roll(th, LANES - 16, 1) * sin_lo + pltpu.roll(th, 16, 1) * sin_hi)
    return jnp.concatenate(outs, axis=-1)


def _inproj_kernel(*refs, rope, cache, q_scale):
    x_ref, mod_ref, g_ref, w_ref, bd_ref = refs[:5]
    pos = 5
    if rope:
        cos_ref, slo_ref, shi_ref = refs[pos:pos + 3]
        pos += 3
    acas_ref, q_ref, k_ref, v_ref = refs[pos:pos + 4]
    m = mod_ref[0]
    h = _norm_mod(x_ref[...], g_ref[...], m[0:1], m[1:2]).astype(BF16)
    fw = bd_ref.shape[0]
    dw = q_ref.shape[1]
    f = _dot(h, w_ref[:, 0:fw])
    acas_ref[...] = _dot(f.astype(BF16), bd_ref[...]).astype(BF16)
    q = _dot(h, w_ref[:, fw:fw + dw])
    k = _dot(h, w_ref[:, fw + dw:fw + 2 * dw])
    v = _dot(h, w_ref[:, fw + 2 * dw:fw + 3 * dw])
    if cache:
        kc_ref, vc_ref = refs[pos + 4:pos + 6]
        kc_ref[...] = k
        vc_ref[...] = v
    if rope:
        cos, slo, shi = cos_ref[...], slo_ref[...], shi_ref[...]
        q = _rope(q, cos, slo, shi)
        k = _rope(k, cos, slo, shi)
    q_ref[...] = (q * q_scale).astype(BF16)
    k_ref[...] = k.astype(BF16)
    v_ref[...] = v.astype(BF16)


def _inproj(x, mod, g, w_in, bd, *, seq, mod_row0, rope_tables, cache, q_scale):
    n, d = x.shape
    tm = min(TOKEN_TILE, seq)
    fw = bd.shape[0]
    dw = (w_in.shape[1] - fw) // 3
    per_seq = seq // tm
    row = lambda i: (i, 0)
    in_specs = [pl.BlockSpec((tm, d), row),
                pl.BlockSpec((1, 6, d), lambda i: (mod_row0 + (i // per_seq if mod_row0 else 0), 0, 0)),
                _const_spec((1, d)), _const_spec(w_in.shape), _const_spec(bd.shape)]
    args = [x, mod, g, w_in, bd]
    if rope_tables is not None:
        in_specs += [pl.BlockSpec((tm, LANES), lambda i: (i % per_seq, 0))] * 3
        args += list(rope_tables)
    out_shape = [jax.ShapeDtypeStruct((n, 2 * fw), BF16)] + [jax.ShapeDtypeStruct((n, dw), BF16)] * 3
    out_specs = [pl.BlockSpec((tm, 2 * fw), row)] + [pl.BlockSpec((tm, dw), row)] * 3
    if cache:
        out_shape += [jax.ShapeDtypeStruct((n, dw), F32)] * 2
        out_specs += [pl.BlockSpec((tm, dw), row)] * 2
    return pl.pallas_call(
        functools.partial(_inproj_kernel, rope=rope_tables is not None, cache=cache, q_scale=q_scale),
        out_shape=out_shape, grid=(n // tm,), in_specs=in_specs, out_specs=out_specs,
        compiler_params=_cparams(("parallel",)),
        name="inproj_ctx" if cache else "inproj_lat",
    )(*args)


def _fourier_short_kernel(ct_ref, st_ref, acas_ref, o_ref, *, scale):
    fw = o_ref.shape[1]
    y = _dot(ct_ref[...], acas_ref[:, 0:fw]) - _dot(st_ref[...], acas_ref[:, fw:2 * fw])
    o_ref[...] = (y * scale).astype(BF16)


def _fourier_short(acas, seq):
    n, fw2 = acas.shape
    fw = fw2 // 2
    idx = (np.arange(seq)[:, None] * np.arange(seq)[None, :]) % seq
    ang = (2.0 * np.pi / seq) * idx
    ct = jnp.asarray(np.cos(ang), BF16)
    st = jnp.asarray(np.sin(ang), BF16)
    scale = 1.0 / math.sqrt(seq * FOURIER_GROUP_DIM)
    return pl.pallas_call(
        functools.partial(_fourier_short_kernel, scale=scale),
        out_shape=jax.ShapeDtypeStruct((n, fw), BF16),
        grid=(n // seq,),
        in_specs=[_const_spec((seq, seq)), _const_spec((seq, seq)),
                  pl.BlockSpec((seq, fw2), lambda b: (b, 0))],
        out_specs=pl.BlockSpec((seq, fw), lambda b: (b, 0)),
        compiler_params=_cparams(("parallel",)),
        name="fourier_ctx",
    )(ct, st, acas)


def _fourier_long_kernel(cb_ref, sb_ref, ca_ref, sa_ref, acas_ref, o_ref, ct_scr, st_scr, *, scale):
    @pl.when(pl.program_id(1) == 0)
    def _():
        tb, seq = cb_ref.shape
        cw = 512
        for part in range(ca_ref.shape[0]):
            rows = slice(part * tb, (part + 1) * tb)
            for c0 in range(0, seq, cw):
                ca, sa = ca_ref[part, :, c0:c0 + cw], sa_ref[part, :, c0:c0 + cw]
                cb, sb = cb_ref[:, c0:c0 + cw], sb_ref[:, c0:c0 + cw]
                ct_scr[rows, c0:c0 + cw] = (ca * cb - sa * sb).astype(BF16)
                st_scr[rows, c0:c0 + cw] = (sa * cb + ca * sb).astype(BF16)
    fw = o_ref.shape[1]
    y = _dot(ct_scr[...], acas_ref[:, 0:fw]) - _dot(st_scr[...], acas_ref[:, fw:2 * fw])
    o_ref[...] = (y * scale).astype(BF16)


def _fourier_long(acas, seq):
    n, fw2 = acas.shape
    fw = fw2 // 2
    nb = n // seq
    tr = min(FOURIER_ROW_TILE, seq)
    tb = min(FOURIER_TABLE_ROWS, tr)
    parts = tr // tb
    nr = seq // tr
    col = np.arange(seq)[None, :]
    beta = (2.0 * np.pi / seq) * ((np.arange(tb)[:, None] * col) % seq)
    alpha = (2.0 * np.pi / seq) * (((np.arange(seq // tb) * tb)[:, None] * col) % seq)
    cb, sb = jnp.asarray(np.cos(beta), F32), jnp.asarray(np.sin(beta), F32)
    ca = jnp.asarray(np.cos(alpha), F32).reshape(seq // tb, 1, seq)
    sa = jnp.asarray(np.sin(alpha), F32).reshape(seq // tb, 1, seq)
    scale = 1.0 / math.sqrt(seq * FOURIER_GROUP_DIM)
    return pl.pallas_call(
        functools.partial(_fourier_long_kernel, scale=scale),
        out_shape=jax.ShapeDtypeStruct((n, fw), BF16),
        grid=(nr, nb),
        in_specs=[_const_spec((tb, seq)), _const_spec((tb, seq)),
                  pl.BlockSpec((parts, 1, seq), lambda r, b: (r, 0, 0)),
                  pl.BlockSpec((parts, 1, seq), lambda r, b: (r, 0, 0)),
                  pl.BlockSpec((seq, fw2), lambda r, b: (b, 0))],
        out_specs=pl.BlockSpec((tr, fw), lambda r, b: (b * nr + r, 0)),
        scratch_shapes=[pltpu.VMEM((tr, seq), BF16), pltpu.VMEM((tr, seq), BF16)],
        compiler_params=_cparams(("arbitrary", "arbitrary")),
        name="fourier_lat",
    )(cb, sb, ca, sa, acas)


def _col_reduce(x, op):
    r = x.reshape(x.shape[0] // SUBLANES, SUBLANES, x.shape[1])
    n = r.shape[0]
    while n > 1:
        n //= 2
        r = op(r[:n], r[n:2 * n])
    r = r[0]
    for shift in (4, 2, 1):
        r = op(r, pltpu.roll(r, shift, 0))
    return r[0:1]


def _attn_kernel(*refs, n_chunks, tk, has_ctx, lam_init):
    lam_ref, sub_ref, q_ref, k_ref, v_ref = refs[:5]
    pos = 5
    if has_ctx:
        kc_ref, vc_ref = refs[5:7]
        pos = 7
    o_ref, vt_scr, acc_scr, s_scr = refs[pos:pos + 4]
    if has_ctx:
        vtc_scr = refs[pos + 4]
    tq, hd2 = q_ref.shape
    half = hd2 // 2
    depth = s_scr.shape[0]

    @pl.when(pl.program_id(2) == 0)
    def _():
        for c in range(n_chunks):
            vt_scr[c] = v_ref[c * tk:(c + 1) * tk, :].astype(F32).T.astype(BF16)
        if has_ctx:
            vtc_scr[...] = vc_ref[...].astype(F32).T.astype(BF16)

    qt = q_ref[...].astype(F32).T
    row = lax.broadcasted_iota(jnp.int32, qt.shape, 0)
    qts = (jnp.where(row < half, qt, 0.0).astype(BF16),
           jnp.where(row >= half, qt, 0.0).astype(BF16))

    chunks = [(k_ref, c * tk, tk, functools.partial(vt_scr.__getitem__, c)) for c in range(n_chunks)]
    if has_ctx:
        chunks.append((kc_ref, 0, kc_ref.shape[0], functools.partial(vtc_scr.__getitem__, Ellipsis)))

    def scores(idx):
        k_src, start, size, _ = chunks[idx]
        kc = k_src[start:start + size, :]
        for j in range(2):
            s_scr[idx % depth, j, 0:size, :] = _dot(kc, qts[j])

    def softmax_pv(idx, state):
        _, _, size, vt = chunks[idx]
        new = []
        for j in range(2):
            s = s_scr[idx % depth, j, 0:size, :]
            m_blk = _col_reduce(s, jnp.maximum)
            if state is None:
                m_new = m_blk
                p = jnp.exp2(s - m_new)
                l_new = _col_reduce(p, jnp.add)
                acc_scr[j] = _dot(vt(), p.astype(BF16))
            else:
                m, l = state[j]
                m_new = jnp.maximum(m, m_blk)
                alpha = jnp.exp2(m - m_new)
                p = jnp.exp2(s - m_new)
                l_new = alpha * l + _col_reduce(p, jnp.add)
                acc_scr[j] = alpha * acc_scr[j] + _dot(vt(), p.astype(BF16))
            new.append((m_new, l_new))
        return tuple(new)

    state = None
    ahead = depth - 1
    for idx in range(min(ahead, len(chunks))):
        scores(idx)
    for idx in range(len(chunks)):
        if idx + ahead < len(chunks):
            scores(idx + ahead)
        state = softmax_pv(idx, state)
    (_, l1), (_, l2) = state
    lp = lam_ref[...]
    lam = (jnp.exp(jnp.sum(lp[0:1] * lp[1:2], axis=-1, keepdims=True))
           - jnp.exp(jnp.sum(lp[2:3] * lp[3:4], axis=-1, keepdims=True)) + lam_init)
    ot = acc_scr[0] * (1.0 / l1) - lam * (acc_scr[1] * (1.0 / l2))
    o = ot.T
    o = o * lax.rsqrt(jnp.mean(o * o, axis=-1, keepdims=True) + SUBLN_EPS)
    o_ref[...] = (o * sub_ref[...] * (1.0 - lam_init)).astype(BF16)


def _attention(lam_p, sub_g, q, k, v, k_ctx, v_ctx, *, seq, lam_init):
    n, dw = q.shape
    nb = n // seq
    hd2 = dw // N_HEADS
    tq = min(ATTN_Q_TILE, seq)
    tk = min(ATTN_K_TILE, seq)
    nq = seq // tq
    has_ctx = k_ctx is not None
    in_specs = [_const_spec(lam_p.shape), _const_spec(sub_g.shape),
                pl.BlockSpec((tq, hd2), lambda b, h, i: (b * nq + i, h)),
                pl.BlockSpec((seq, hd2), lambda b, h, i: (b, h)),
                pl.BlockSpec((seq, hd2), lambda b, h, i: (b, h))]
    args = [lam_p, sub_g, q, k, v]
    scratch = [pltpu.VMEM((seq // tk, hd2, tk), BF16), pltpu.VMEM((2, hd2, tq), F32),
               pltpu.VMEM((ATTN_SCORE_BUFFERS, 2, tk, tq), F32)]
    if has_ctx:
        lc = k_ctx.shape[0] // nb
        in_specs += [pl.BlockSpec((lc, hd2), lambda b, h, i: (b, h))] * 2
        args += [k_ctx, v_ctx]
        scratch.append(pltpu.VMEM((hd2, lc), BF16))
    return pl.pallas_call(
        functools.partial(_attn_kernel, n_chunks=seq // tk, tk=tk, has_ctx=has_ctx, lam_init=lam_init),
        out_shape=jax.ShapeDtypeStruct((n, dw), BF16),
        grid=(nb, N_HEADS, nq),
        in_specs=in_specs,
        out_specs=pl.BlockSpec((tq, hd2), lambda b, h, i: (b * nq + i, h)),
        scratch_shapes=scratch,
        compiler_params=_cparams(("parallel", "parallel", "arbitrary")),
        name="attn_lat" if has_ctx else "attn_ctx",
    )(*args)


def _outproj_ffn_kernel(xc_ref, fc_ref, ac_ref, xl_ref, fl_ref, al_ref, wo_ref, mod_ref, g_ref, w1_ref, w3_ref,
                        w2_ref, o_ref, x_scr, *, ctx_tiles):
    m = mod_ref[0]

    def residual(x_ref, four_ref, attn_ref):
        fw = four_ref.shape[1]
        y = _dot(four_ref[...], wo_ref[0:fw, :]) + _dot(attn_ref[...], wo_ref[fw:, :])
        x_scr[...] = x_ref[...] + m[2:3] * y

    @pl.when(pl.program_id(0) < ctx_tiles)
    def _():
        residual(xc_ref, fc_ref, ac_ref)

    @pl.when(pl.program_id(0) >= ctx_tiles)
    def _():
        residual(xl_ref, fl_ref, al_ref)

    x = x_scr[...]
    h = _norm_mod(x, g_ref[...], m[3:4], m[4:5]).astype(BF16)
    a = (_silu(_dot(h, w1_ref[...])) * _dot(h, w3_ref[...])).astype(BF16)
    o_ref[...] = x + m[5:6] * _dot(a, w2_ref[...])


def _outproj_ffn(ctx, lat, w_out, mod, g, w1, w3, w2, *, seq):
    n_ctx, d = ctx[0].shape
    n_lat = lat[0].shape[0]
    tm = TOKEN_TILE
    tc = n_ctx // tm
    ctx_row = lambda i: (jnp.minimum(i, tc - 1), 0)
    lat_row = lambda i: (jnp.maximum(i - tc, 0), 0)
    in_specs = ([pl.BlockSpec((tm, a.shape[1]), ctx_row) for a in ctx]
                + [pl.BlockSpec((tm, a.shape[1]), lat_row) for a in lat]
                + [_const_spec(w_out.shape), pl.BlockSpec((1, 6, d), _seg_index_map(tm, n_ctx, seq)),
                   _const_spec((1, d)), _const_spec(w1.shape), _const_spec(w3.shape), _const_spec(w2.shape)])
    return pl.pallas_call(
        functools.partial(_outproj_ffn_kernel, ctx_tiles=tc),
        out_shape=jax.ShapeDtypeStruct((n_ctx + n_lat, d), F32),
        grid=((n_ctx + n_lat) // tm,), in_specs=in_specs,
        out_specs=pl.BlockSpec((tm, d), lambda i: (i, 0)),
        scratch_shapes=[pltpu.VMEM((tm, d), F32)],
        compiler_params=_cparams(("parallel",)),
        name="outproj_ffn",
    )(*ctx, *lat, w_out, mod, g, w1, w3, w2)


def _conv_kernel(x_ref, prev_ref, next_ref, g_ref, w1_ref, wdw_ref, bdw_ref, lng_ref, lnb_ref, w2_ref, mod_ref,
                 o_ref, win_scr, conv_scr, shift_scr, *, n_ctx, seq_ctx, seq_lat, taps, front):
    tm, d = x_ref.shape
    start = pl.program_id(0) * tm
    is_ctx = start < n_ctx
    off = jnp.where(is_ctx, start % seq_ctx, (start - n_ctx) % seq_lat)
    slen = jnp.where(is_ctx, seq_ctx, seq_lat)
    m = mod_ref[0]
    h = jnp.concatenate([_norm_mod(r[...], g_ref[...], m[0:1], m[1:2]).astype(BF16)
                         for r in (prev_ref, x_ref, next_ref)], axis=0)
    for cb in range(d // LANES):
        vg = _dot(h, w1_ref[:, 2 * cb * LANES:2 * (cb + 1) * LANES])
        u = vg[:, 0:LANES] * jax.nn.sigmoid(vg[:, LANES:2 * LANES])
        cols = slice(cb * LANES, (cb + 1) * LANES)
        win_scr[0:CONV_HALO, cols] = jnp.where(off == 0, 0.0, u[0:CONV_HALO])
        win_scr[CONV_HALO:CONV_HALO + tm, cols] = u[CONV_HALO:CONV_HALO + tm]
        win_scr[CONV_HALO + tm:, cols] = jnp.where(off + tm == slen, 0.0, u[CONV_HALO + tm:])
    n_q = wdw_ref.shape[0]
    rc = CONV_ROW_CHUNK
    for cb in range(d // LANES):
        cols = slice(cb * LANES, (cb + 1) * LANES)
        for k in range(1, SUBLANES):
            shift_scr[k] = win_scr[k:k + tm + n_q - SUBLANES, cols]
        w = {q: wdw_ref[q, :, cols] for q in range(front, front + taps)}
        bias = bdw_ref[:, cols]
        for r0 in range(0, tm, rc):
            acc = None
            for q in range(front, front + taps):
                k = q % SUBLANES
                if k:
                    tap = shift_scr[k, r0 + q - k:r0 + q - k + rc, :]
                else:
                    tap = win_scr[r0 + q:r0 + q + rc, cols]
                term = tap.reshape(rc // SUBLANES, SUBLANES, LANES) * w[q][None]
                acc = term if acc is None else acc + term
            conv_scr[r0:r0 + rc, cols] = acc.reshape(rc, LANES) + bias
    c = conv_scr[...]
    mu = jnp.mean(c, axis=-1, keepdims=True)
    cc = c - mu
    var = jnp.mean(cc * cc, axis=-1, keepdims=True)
    y = _silu(cc * lax.rsqrt(var + NORM_EPS) * lng_ref[...] + lnb_ref[...]).astype(BF16)
    o_ref[...] = x_ref[...] + m[2:3] * _dot(y, w2_ref[...])


def _conv(x_all, mod, g, w_pw1, w_dw, b_dw, ln_g, ln_b, w_pw2, *, n_ctx, seq_ctx, seq_lat):
    n, d = x_all.shape
    tm = min(CONV_TILE, seq_ctx)
    w_pw1 = w_pw1.reshape(d, 2, d // LANES, LANES).transpose(0, 2, 1, 3).reshape(d, 2 * d)
    taps = w_dw.shape[0]
    front = CONV_HALO - taps // 2
    w_dw = jnp.pad(w_dw, ((front, 2 * CONV_HALO - taps - front), (0, 0)))
    w_dw = jnp.broadcast_to(w_dw[:, None, :], (2 * CONV_HALO, SUBLANES, d))
    hb = tm // CONV_HALO
    last = n // CONV_HALO - 1
    row = lambda i: (i, 0)
    return pl.pallas_call(
        functools.partial(_conv_kernel, n_ctx=n_ctx, seq_ctx=seq_ctx, seq_lat=seq_lat, taps=taps, front=front),
        out_shape=jax.ShapeDtypeStruct((n, d), F32),
        grid=(n // tm,),
        in_specs=[pl.BlockSpec((tm, d), row),
                  pl.BlockSpec((CONV_HALO, d), lambda i: (jnp.maximum(i * hb - 1, 0), 0)),
                  pl.BlockSpec((CONV_HALO, d), lambda i: (jnp.minimum((i + 1) * hb, last), 0)),
                  _const_spec((1, d)), _const_spec(w_pw1.shape),
                  _const_spec(w_dw.shape), _const_spec((1, d)), _const_spec((1, d)), _const_spec((1, d)),
                  _const_spec(w_pw2.shape),
                  pl.BlockSpec((1, 6, d), _seg_index_map(tm, n_ctx, seq_lat))],
        out_specs=pl.BlockSpec((tm, d), row),
        scratch_shapes=[pltpu.VMEM((tm + 2 * CONV_HALO, d), F32), pltpu.VMEM((tm, d), F32),
                        pltpu.VMEM((SUBLANES, tm + 2 * CONV_HALO - SUBLANES, LANES), F32)],
        compiler_params=_cparams(("parallel",)),
        name="conv_module",
    )(x_all, x_all, x_all, g, w_pw1, w_dw, b_dw, ln_g, ln_b, w_pw2, mod)


def _router_kernel(x_ref, mod_ref, g_ref, wr_ref, tri_ref, upper_ref, h_ref, info_ref, runs_ref, *, n_experts):
    m = mod_ref[0]
    h = _norm_mod(x_ref[...], g_ref[...], m[3:4], m[4:5])
    h_hi = h.astype(BF16)
    h_ref[...] = h_hi
    h_lo = (h - h_hi.astype(F32)).astype(BF16)
    wr = wr_ref[...]
    w_hi = wr.astype(BF16)
    w_lo = (wr - w_hi.astype(F32)).astype(BF16)
    logits = _dot(h_hi, w_hi) + (_dot(h_lo, w_hi) + _dot(h_hi, w_lo))
    lane = lax.broadcasted_iota(jnp.int32, logits.shape, 1)
    neg = jnp.float32(-jnp.inf)
    logits = jnp.where(lane < n_experts, logits, neg)
    m1 = jnp.max(logits, axis=-1, keepdims=True)
    i1 = jnp.min(jnp.where(logits == m1, lane, LANES), axis=-1, keepdims=True)
    oh1 = lane == i1
    rest = jnp.where(oh1, neg, logits)
    m2 = jnp.max(rest, axis=-1, keepdims=True)
    i2 = jnp.min(jnp.where(rest == m2, lane, LANES), axis=-1, keepdims=True)
    oh2 = lane == i2
    e = jnp.exp(m2 - m1)
    g1 = 1.0 / (1.0 + e)
    g2 = e * g1
    oh = jnp.where(oh1, 1.0, 0.0) + jnp.where(oh2, 1.0, 0.0)
    cnt = jnp.sum(oh, axis=0, keepdims=True)
    run_units = jnp.floor((cnt + (RUN_ALIGN - 1.0)) * (1.0 / RUN_ALIGN))
    units8 = jnp.broadcast_to(run_units, (SUBLANES, LANES)).astype(BF16)
    run_start = _dot(units8, upper_ref[...])[0:1, :] * RUN_ALIGN
    before = _dot(tri_ref[...], oh.astype(BF16)) + run_start
    p1 = jnp.sum(jnp.where(oh1, before, 0.0), axis=-1, keepdims=True)
    p2 = jnp.sum(jnp.where(oh2, before, 0.0), axis=-1, keepdims=True)
    info = jnp.where(lane == 0, p1,
           jnp.where(lane == 1, p2,
           jnp.where(lane == 2, g1,
           jnp.where(lane == 3, g2, 0.0))))
    info_ref[...] = info
    sub = lax.broadcasted_iota(jnp.int32, (SUBLANES, LANES), 0)
    runs_ref[0] = jnp.where(sub == 0, run_units * RUN_ALIGN, jnp.where(sub == 1, run_start, 0.0))


def _router(x_all, mod, g, w_router, *, n_ctx, seq):
    n, d = x_all.shape
    n_experts = w_router.shape[1]
    tm = TOKEN_TILE
    wr = jnp.pad(w_router, ((0, 0), (0, LANES - n_experts)))
    tri = jnp.asarray(np.tril(np.ones((tm, tm), np.float32), -1), BF16)
    upper = jnp.asarray(np.triu(np.ones((LANES, LANES), np.float32), 1), BF16)
    row = lambda i: (i, 0)
    return pl.pallas_call(
        functools.partial(_router_kernel, n_experts=n_experts),
        out_shape=[jax.ShapeDtypeStruct((n, d), BF16), jax.ShapeDtypeStruct((n, LANES), F32),
                   jax.ShapeDtypeStruct((n // tm, SUBLANES, LANES), F32)],
        grid=(n // tm,),
        in_specs=[pl.BlockSpec((tm, d), row), pl.BlockSpec((1, 6, d), _seg_index_map(tm, n_ctx, seq)),
                  _const_spec((1, d)), _const_spec(wr.shape), _const_spec(tri.shape), _const_spec(upper.shape)],
        out_specs=[pl.BlockSpec((tm, d), row), pl.BlockSpec((tm, LANES), row),
                   pl.BlockSpec((1, SUBLANES, LANES), lambda i: (i, 0, 0))],
        compiler_params=_cparams(("parallel",)),
        name="moe_router",
    )(x_all, mod, g, wr, tri, upper)


def _run_copies(n_experts, len_ref, start_ref, dst_ref, tile, make_copy):
    ops = []
    for e in range(n_experts):
        n = len_ref[tile * n_experts + e]
        src = start_ref[tile * n_experts + e]
        dst = dst_ref[tile * n_experts + e]
        ops.append((n > 0, make_copy(pl.multiple_of(src, RUN_ALIGN), pl.multiple_of(dst, RUN_ALIGN),
                                     pl.multiple_of(n, RUN_ALIGN))))
    return ops


def _dispatch_kernel(zrow_ref, len_ref, start_ref, dst_ref, h_ref, info_ref, xg_ref, zero_scr, xs_scr, sem, zsem,
                     *, n_zero, n_experts):
    tm = h_ref.shape[0]
    tz = zero_scr.shape[0]
    rows = xs_scr.shape[1]

    def zero_copy(e):
        return pltpu.make_async_copy(zero_scr, xg_ref.at[pl.ds(pl.multiple_of(zrow_ref[e], tz), tz)], zsem)

    @pl.when(pl.program_id(0) == 0)
    def _():
        zero_scr[...] = jnp.zeros_like(zero_scr)
        for e in range(n_zero):
            @pl.when(zrow_ref[e] >= 0)
            def _():
                zero_copy(e).start()
        for e in range(n_zero):
            @pl.when(zrow_ref[e] >= 0)
            def _():
                zero_copy(e).wait()

    info_t = info_ref[...].T
    slot1 = info_t[0:1, :].astype(jnp.int32)
    slot2 = info_t[1:2, :].astype(jnp.int32)
    r = lax.broadcasted_iota(jnp.int32, (rows, tm), 0)
    perm = (jnp.where(r == slot1, 1.0, 0.0) + jnp.where(r == slot2, 1.0, 0.0)).astype(BF16)
    step = pl.program_id(0)
    buf = step % 2
    xs_scr[buf] = _dot(perm, h_ref[...])

    def run_writes(tile, b):
        def make_copy(src, dst, size):
            return pltpu.make_async_copy(xs_scr.at[b, pl.ds(src, size)], xg_ref.at[pl.ds(dst, size)], sem.at[b])
        return _run_copies(n_experts, len_ref, start_ref, dst_ref, tile, make_copy)

    for cond, copy in run_writes(step, buf):
        pl.when(cond)(copy.start)

    @pl.when(step > 0)
    def _():
        for cond, copy in run_writes(step - 1, 1 - buf):
            pl.when(cond)(copy.wait)

    @pl.when(step == pl.num_programs(0) - 1)
    def _():
        for cond, copy in run_writes(step, buf):
            pl.when(cond)(copy.wait)


def _dispatch(h_all, info, run_len, run_start, run_dst, zero_tile_row, n_rows):
    n, d = h_all.shape
    tm = TOKEN_TILE
    n_zero = zero_tile_row.shape[0]
    n_experts = run_len.shape[0] // (n // tm)
    grid_spec = pltpu.PrefetchScalarGridSpec(
        num_scalar_prefetch=4,
        grid=(n // tm,),
        in_specs=[pl.BlockSpec((tm, d), lambda i, *_: (i, 0)),
                  pl.BlockSpec((tm, LANES), lambda i, *_: (i, 0))],
        out_specs=pl.BlockSpec(memory_space=pl.ANY),
        scratch_shapes=[pltpu.VMEM((MOE_TILE, d), F32), pltpu.VMEM((2, _local_rows(tm, n_experts), d), F32),
                        pltpu.SemaphoreType.DMA((2,)), pltpu.SemaphoreType.DMA(())],
    )
    return pl.pallas_call(
        functools.partial(_dispatch_kernel, n_zero=n_zero, n_experts=n_experts),
        out_shape=jax.ShapeDtypeStruct((n_rows, d), F32),
        grid_spec=grid_spec,
        compiler_params=_cparams(("arbitrary",)),
        name="moe_dispatch",
    )(zero_tile_row, run_len, run_start, run_dst, h_all, info)


def _experts_kernel(te_ref, nt_ref, x_ref, w1_ref, w3_ref, w2_ref, o_ref, xb_scr):
    c = pl.program_id(1)

    @pl.when(pl.program_id(0) < nt_ref[0])
    def _():
        @pl.when(c == 0)
        def _():
            xb_scr[...] = x_ref[...].astype(BF16)
        xb = xb_scr[...]
        a = (_silu(_dot(xb, w1_ref[0])) * _dot(xb, w3_ref[0])).astype(BF16)
        y = _dot(a, w2_ref[0])

        @pl.when(c == 0)
        def _():
            o_ref[...] = y

        @pl.when(c > 0)
        def _():
            o_ref[...] += y

    @pl.when(jnp.logical_and(pl.program_id(0) >= nt_ref[0], c == 0))
    def _():
        o_ref[...] = jnp.zeros_like(o_ref)


def _experts(xg, w1, w3, w2, tile_expert, n_tiles_used):
    p, d = xg.shape
    tm = MOE_TILE
    ff = w1.shape[2]
    fc = MOE_FF_CHUNK if ff % MOE_FF_CHUNK == 0 else ff
    nc = ff // fc
    n_tiles = p // tm

    def tile(t, nt):
        return jnp.minimum(t, nt[0] - 1)

    def chunk(t, c, nt):
        return jnp.where(t < nt[0], c, nc - 1)

    grid_spec = pltpu.PrefetchScalarGridSpec(
        num_scalar_prefetch=2,
        grid=(n_tiles, nc),
        in_specs=[pl.BlockSpec((tm, d), lambda t, c, te, nt: (tile(t, nt), 0)),
                  pl.BlockSpec((1, d, fc), lambda t, c, te, nt: (te[tile(t, nt)], 0, chunk(t, c, nt))),
                  pl.BlockSpec((1, d, fc), lambda t, c, te, nt: (te[tile(t, nt)], 0, chunk(t, c, nt))),
                  pl.BlockSpec((1, fc, d), lambda t, c, te, nt: (te[tile(t, nt)], chunk(t, c, nt), 0))],
        out_specs=pl.BlockSpec((tm, d), lambda t, c, te, nt: (t, 0)),
        scratch_shapes=[pltpu.VMEM((tm, d), BF16)],
    )
    return pl.pallas_call(
        _experts_kernel,
        out_shape=jax.ShapeDtypeStruct((p, d), F32),
        grid_spec=grid_spec,
        compiler_params=_cparams(("arbitrary", "arbitrary")),
        name="moe_experts",
    )(tile_expert, n_tiles_used, xg, w1, w3, w2)


def _combine_kernel(len_ref, start_ref, dst_ref, yg_ref, x_ref, info_ref, mod_ref, fg_ref, o_ref, ys_scr, sem,
                    *, n_experts, tile0):
    tm = x_ref.shape[0]
    rows = ys_scr.shape[1]
    step = pl.program_id(0)
    buf = step % 2

    def run_reads(tile, b):
        def make_copy(src, dst, size):
            return pltpu.make_async_copy(yg_ref.at[pl.ds(dst, size)], ys_scr.at[b, pl.ds(src, size)], sem.at[b])
        return _run_copies(n_experts, len_ref, start_ref, dst_ref, tile0 + tile, make_copy)

    @pl.when(step == 0)
    def _():
        ys_scr[...] = jnp.zeros_like(ys_scr)
        for cond, copy in run_reads(step, buf):
            pl.when(cond)(copy.start)

    @pl.when(step + 1 < pl.num_programs(0))
    def _():
        for cond, copy in run_reads(step + 1, 1 - buf):
            pl.when(cond)(copy.start)

    for cond, copy in run_reads(step, buf):
        pl.when(cond)(copy.wait)
    info = info_ref[...]
    ys = ys_scr[buf].astype(BF16)
    lane = lax.broadcasted_iota(jnp.int32, (tm, rows), 1)
    pick1 = jnp.where(lane == info[:, 0:1].astype(jnp.int32), 1.0, 0.0).astype(BF16)
    pick2 = jnp.where(lane == info[:, 1:2].astype(jnp.int32), 1.0, 0.0).astype(BF16)
    y = info[:, 2:3] * _dot(pick1, ys) + info[:, 3:4] * _dot(pick2, ys)
    x = x_ref[...] + mod_ref[0][5:6] * y
    o_ref[...] = x * lax.rsqrt(jnp.mean(x * x, axis=-1, keepdims=True) + NORM_EPS) * fg_ref[...]


def _combine(yg, x_all, info, run_len, run_start, run_dst, mod, final_g, *, row0, n_rows, seq, mod_row0):
    n, d = x_all.shape
    tm = TOKEN_TILE
    t0 = row0 // tm
    per_seq = seq // tm
    n_experts = run_len.shape[0] // (n // tm)
    grid_spec = pltpu.PrefetchScalarGridSpec(
        num_scalar_prefetch=3,
        grid=(n_rows // tm,),
        in_specs=[pl.BlockSpec(memory_space=pl.ANY),
                  pl.BlockSpec((tm, d), lambda i, *_: (t0 + i, 0)),
                  pl.BlockSpec((tm, LANES), lambda i, *_: (t0 + i, 0)),
                  pl.BlockSpec((1, 6, d), lambda i, *_: (mod_row0 + (i // per_seq if mod_row0 else 0), 0, 0)),
                  pl.BlockSpec((1, d), lambda i, *_: (0, 0))],
        out_specs=pl.BlockSpec((tm, d), lambda i, *_: (i, 0)),
        scratch_shapes=[pltpu.VMEM((2, _local_rows(tm, n_experts), d), F32), pltpu.SemaphoreType.DMA((2,))],
    )
    return pl.pallas_call(
        functools.partial(_combine_kernel, n_experts=n_experts, tile0=t0),
        out_shape=jax.ShapeDtypeStruct((n_rows, d), F32),
        grid_spec=grid_spec,
        compiler_params=_cparams(("arbitrary",)),
        name="moe_combine",
    )(run_len, run_start, run_dst, yg, x_all, info, mod, final_g)


def _moe(x_all, mod, g, w_router, w1, w3, w2, final_g, *, n_ctx, seq_ctx_total, seq_lat):
    n, d = x_all.shape
    n_experts = w_router.shape[1]
    tm = TOKEN_TILE
    h_all, info, runs = _router(x_all, mod, g, w_router, n_ctx=n_ctx, seq=seq_lat)
    run_len = runs[:, 0, :n_experts].astype(jnp.int32)
    run_start = runs[:, 1, :n_experts].astype(jnp.int32)
    cnt = jnp.sum(run_len, axis=0)
    padded = ((cnt + MOE_TILE - 1) // MOE_TILE) * MOE_TILE
    ends = jnp.cumsum(padded)
    offs = ends - padded
    run_dst = offs[None, :] + jnp.cumsum(run_len, axis=0) - run_len
    n_tiles = -(-(TOP_K * n + (n // tm) * n_experts * (RUN_ALIGN - 1)) // MOE_TILE) + n_experts
    n_used = (ends[-1] // MOE_TILE).astype(jnp.int32)
    tile_start = jnp.arange(n_tiles, dtype=jnp.int32) * MOE_TILE
    tile_expert = jnp.minimum(jnp.sum(tile_start[:, None] >= ends[None, :], axis=1), n_experts - 1).astype(jnp.int32)
    tile_expert = jnp.where(tile_start < ends[-1], tile_expert, tile_expert[jnp.maximum(n_used - 1, 0)])
    tail_start = tile_start[(TOP_K * n) // MOE_TILE:]
    zero_tile_row = jnp.concatenate([jnp.where(padded > 0, ends - MOE_TILE, -1),
                                     jnp.where(tail_start >= ends[-1], tail_start, -1)]).astype(jnp.int32)
    run_args = (run_len.reshape(-1), run_start.reshape(-1), run_dst.reshape(-1).astype(jnp.int32))
    xg = _dispatch(h_all, info, *run_args, zero_tile_row, n_tiles * MOE_TILE)
    yg = _experts(xg, w1, w3, w2, tile_expert, n_used.reshape(1))
    y_ctx = _combine(yg, x_all, info, *run_args, mod, final_g, row0=0, n_rows=n_ctx, seq=seq_ctx_total, mod_row0=0)
    y_lat = _combine(yg, x_all, info, *run_args, mod, final_g, row0=n_ctx, n_rows=n - n_ctx, seq=seq_lat,
                     mod_row0=1)
    return y_ctx, y_lat


def _rope_tables(seq, hd):
    axis_dim = hd // 2
    nf = axis_dim // 2
    inv = jnp.power(ROPE_BASE, -jnp.arange(nf, dtype=F32) / nf)
    pos = np.arange(seq)
    row = jnp.asarray(pos // GRID_W, F32)
    col = jnp.asarray(pos % GRID_W, F32)
    lane = np.arange(2 * hd)
    dd = lane % hd
    use_col = (dd // axis_dim) == 1
    e = dd % axis_dim
    first = jnp.asarray(e < nf)[None, :]
    freq = inv[e % nf]
    ang = jnp.where(jnp.asarray(use_col)[None, :], col[:, None], row[:, None]) * freq[None, :]
    cos, sin = jnp.cos(ang), jnp.sin(ang)
    return cos, jnp.where(first, -sin, 0.0), jnp.where(first, 0.0, sin)


def _channel_dft(fw):
    g = FOURIER_GROUP_DIM
    ang = (2.0 * np.pi / g) * ((np.arange(g)[:, None] * np.arange(g)[None, :]) % g)
    eye = np.eye(fw // g)
    return jnp.asarray(np.concatenate([np.kron(eye, np.cos(ang)), np.kron(eye, np.sin(ang))], axis=1), BF16)


def kernel(x_prompt, x_sample, cache_k, cache_v, c, c_ctx, w_ada, b_ada, norm_g, final_g, w_in_ab, w_out_ab,
           diff_lambda, diff_subln, conv_w_pw1, conv_w_dw, conv_b_dw, conv_ln_g, conv_ln_b, conv_w_pw2,
           ffn_w1, ffn_w3, ffn_w2, moe_router, moe_w1, moe_w3, moe_w2):
    bp, lp, d = x_prompt.shape
    bs, ls, _ = x_sample.shape
    lc = cache_k.shape[2]
    n_ctx, n_lat = bp * lp, bs * ls
    n_all = n_ctx + n_lat
    dw = N_HEADS * cache_k.shape[-1]
    hd = cache_k.shape[-1] // 2
    fw = w_in_ab.shape[2] - 3 * dw

    rows = -(-(1 + bs) // SUBLANES) * SUBLANES
    cond = jnp.zeros((rows, d), F32).at[0].set(c_ctx).at[1:1 + bs].set(c)
    mod = _modulation(cond, w_ada, b_ada)

    lam_init = 0.8 - 0.6 * math.exp(-0.3 * 0)
    q_scale = (hd ** -0.5) * LOG2E
    w_in = w_in_ab[0].astype(BF16)
    bd = _channel_dft(fw)
    g00 = norm_g[0, 0].reshape(1, d)
    xp = x_prompt.reshape(n_ctx, d)
    xs = x_sample.reshape(n_lat, d)
    acas_p, q_p, k_p, v_p, kc_new, vc_new = _inproj(
        xp, mod[0], g00, w_in, bd, seq=lp, mod_row0=0, rope_tables=None, cache=True, q_scale=q_scale)
    acas_s, q_s, k_s, v_s = _inproj(
        xs, mod[0], g00, w_in, bd, seq=ls, mod_row0=1, rope_tables=_rope_tables(ls, hd), cache=False, q_scale=q_scale)
    four_p = _fourier_short(acas_p, lp)
    four_s = _fourier_long(acas_s, ls)
    lam_p = diff_lambda[0]
    sub_g = diff_subln[0].reshape(1, 2 * hd)
    attn_p = _attention(lam_p, sub_g, q_p, k_p, v_p, None, None, seq=lp, lam_init=lam_init)
    k_ctx = cache_k[:, 0].reshape(bs * lc, dw).astype(BF16)
    v_ctx = cache_v[:, 0].reshape(bs * lc, dw).astype(BF16)
    attn_s = _attention(lam_p, sub_g, q_s, k_s, v_s, k_ctx, v_ctx, seq=ls, lam_init=lam_init)
    w_out = w_out_ab[0].astype(BF16)
    x_all = _outproj_ffn((xp, four_p, attn_p), (xs, four_s, attn_s), w_out, mod[0], norm_g[0, 1].reshape(1, d),
                         ffn_w1[0].astype(BF16), ffn_w3[0].astype(BF16), ffn_w2[0].astype(BF16), seq=ls)

    x_all = _conv(x_all, mod[1], norm_g[1, 0].reshape(1, d), conv_w_pw1[0].astype(BF16), conv_w_dw[0],
                  conv_b_dw[0].reshape(1, d), conv_ln_g[0].reshape(1, d), conv_ln_b[0].reshape(1, d),
                  conv_w_pw2[0].astype(BF16), n_ctx=n_ctx, seq_ctx=lp, seq_lat=ls)
    y_ctx, y_lat = _moe(x_all, mod[1], norm_g[1, 1].reshape(1, d), moe_router[0], moe_w1[0].astype(BF16),
                        moe_w3[0].astype(BF16), moe_w2[0].astype(BF16), final_g.reshape(1, d),
                        n_ctx=n_ctx, seq_ctx_total=n_ctx, seq_lat=ls)

    n_even = w_in_ab.shape[0]
    new_k = kc_new.reshape(bp, n_even, lp, N_HEADS, 2 * hd)
    new_v = vc_new.reshape(bp, n_even, lp, N_HEADS, 2 * hd)
    return (y_ctx.reshape(bp, lp, d), y_lat.reshape(bs, ls, d), new_k, new_v)
```

```python
import functools
import math

import numpy as np
import jax
import jax.numpy as jnp
from jax import lax
from jax.experimental import pallas as pl
from jax.experimental.pallas import tpu as pltpu

F32 = jnp.float32
BF16 = jnp.bfloat16

GRID_W = 64
ROPE_BASE = 10000.0
N_HEADS = 4
FOURIER_GROUP_DIM = 128
CONV_HALO = 16
TOP_K = 2
NORM_EPS = 1e-6
SUBLN_EPS = 1e-5
LANES = 128
SUBLANES = 8
LOG2E = 1.4426950408889634

TOKEN_TILE = 512
CONV_TILE = 256
CONV_ROW_CHUNK = 64
ATTN_Q_TILE = 512
ATTN_K_TILE = 512
ATTN_SCORE_BUFFERS = 2
FOURIER_ROW_TILE = 512
FOURIER_TABLE_ROWS = 256
MOE_TILE = 512
MOE_FF_CHUNK = 3584
RUN_ALIGN = SUBLANES
VMEM_LIMIT = 56 * 1024 * 1024
MOE_VMEM_LIMIT = 62 * 1024 * 1024


def _local_rows(tile, n_experts):
    return -(-(TOP_K * tile + n_experts * (RUN_ALIGN - 1)) // LANES) * LANES


def _cparams(semantics, vmem=VMEM_LIMIT):
    return pltpu.CompilerParams(dimension_semantics=semantics, vmem_limit_bytes=vmem)


def _const_spec(shape):
    nd = len(shape)
    return pl.BlockSpec(shape, lambda *_: (0,) * nd, pipeline_mode=pl.Buffered(1))


def _dot(a, b):
    return jnp.dot(a, b, preferred_element_type=F32)


def _silu(x):
    return x * jax.nn.sigmoid(x)


def _norm_mod(x, g, shift, scale):
    ms = jnp.mean(x * x, axis=-1, keepdims=True)
    return (x * lax.rsqrt(ms + NORM_EPS)) * (g * (1.0 + scale)) + shift


def _seg_index_map(tile, n_ctx, seq):
    def index_map(i, *_):
        start = i * tile
        return (jnp.where(start < n_ctx, 0, 1 + (start - n_ctx) // seq), 0, 0)
    return index_map


def _mod_kernel(c_ref, w_ref, b_ref, o_ref):
    c = c_ref[...]
    o_ref[0] = _dot(_silu(c).astype(BF16), w_ref[0].astype(BF16)) + b_ref[0]


def _modulation(cond, w_ada, b_ada):
    depth, d, n6 = w_ada.shape
    rows = cond.shape[0]
    tn = 1536
    out = pl.pallas_call(
        _mod_kernel,
        out_shape=jax.ShapeDtypeStruct((depth, rows, n6), F32),
        grid=(depth, n6 // tn),
        in_specs=[pl.BlockSpec((rows, d), lambda l, j: (0, 0)),
                  pl.BlockSpec((1, d, tn), lambda l, j: (l, 0, j)),
                  pl.BlockSpec((1, 1, tn), lambda l, j: (l, 0, j))],
        out_specs=pl.BlockSpec((1, rows, tn), lambda l, j: (l, 0, j)),
        compiler_params=_cparams(("parallel", "parallel")),
        name="modulation",
    )(cond, w_ada, b_ada.reshape(depth, 1, n6))
    return out.reshape(depth, rows, 6, d)


def _rope(t, cos, sin_lo, sin_hi):
    outs = []
    for h in range(t.shape[1] // LANES):
        th = t[:, h * LANES:(h + 1) * LANES]
        outs.append(th * cos + pltpu.roll(th, LANES - 16, 1) * sin_lo + pltpu.roll(th, 16, 1) * sin_hi)
    return jnp.concatenate(outs, axis=-1)


def _inproj_kernel(*refs, rope, cache, q_scale):
    x_ref, mod_ref, g_ref, w_ref, bd_ref = refs[:5]
    pos = 5
    if rope:
        cos_ref, slo_ref, shi_ref = refs[pos:pos + 3]
        pos += 3
    acas_ref, q_ref, k_ref, v_ref = refs[pos:pos + 4]
    m = mod_ref[0]
    h = _norm_mod(x_ref[...], g_ref[...], m[0:1], m[1:2]).astype(BF16)
    fw = bd_ref.shape[0]
    dw = q_ref.shape[1]
    f = _dot(h, w_ref[:, 0:fw])
    acas_ref[...] = _dot(f.astype(BF16), bd_ref[...]).astype(BF16)
    q = _dot(h, w_ref[:, fw:fw + dw])
    k = _dot(h, w_ref[:, fw + dw:fw + 2 * dw])
    v = _dot(h, w_ref[:, fw + 2 * dw:fw + 3 * dw])
    if cache:
        kc_ref, vc_ref = refs[pos + 4:pos + 6]
        kc_ref[...] = k
        vc_ref[...] = v
    if rope:
        cos, slo, shi = cos_ref[...], slo_ref[...], shi_ref[...]
        q = _rope(q, cos, slo, shi)
        k = _rope(k, cos, slo, shi)
    q_ref[...] = (q * q_scale).astype(BF16)
    k_ref[...] = k.astype(BF16)
    v_ref[...] = v.astype(BF16)


def _inproj(x, mod, g, w_in, bd, *, seq, mod_row0, rope_tables, cache, q_scale):
    n, d = x.shape
    tm = min(TOKEN_TILE, seq)
    fw = bd.shape[0]
    dw = (w_in.shape[1] - fw) // 3
    per_seq = seq // tm
    row = lambda i: (i, 0)
    in_specs = [pl.BlockSpec((tm, d), row),
                pl.BlockSpec((1, 6, d), lambda i: (mod_row0 + (i // per_seq if mod_row0 else 0), 0, 0)),
                _const_spec((1, d)), _const_spec(w_in.shape), _const_spec(bd.shape)]
    args = [x, mod, g, w_in, bd]
    if rope_tables is not None:
        in_specs += [pl.BlockSpec((tm, LANES), lambda i: (i % per_seq, 0))] * 3
        args += list(rope_tables)
    out_shape = [jax.ShapeDtypeStruct((n, 2 * fw), BF16)] + [jax.ShapeDtypeStruct((n, dw), BF16)] * 3
    out_specs = [pl.BlockSpec((tm, 2 * fw), row)] + [pl.BlockSpec((tm, dw), row)] * 3
    if cache:
        out_shape += [jax.ShapeDtypeStruct((n, dw), F32)] * 2
        out_specs += [pl.BlockSpec((tm, dw), row)] * 2
    return pl.pallas_call(
        functools.partial(_inproj_kernel, rope=rope_tables is not None, cache=cache, q_scale=q_scale),
        out_shape=out_shape, grid=(n // tm,), in_specs=in_specs, out_specs=out_specs,
        compiler_params=_cparams(("parallel",)),
        name="inproj_ctx" if cache else "inproj_lat",
    )(*args)


def _fourier_short_kernel(ct_ref, st_ref, acas_ref, o_ref, *, scale):
    fw = o_ref.shape[1]
    y = _dot(ct_ref[...], acas_ref[:, 0:fw]) - _dot(st_ref[...], acas_ref[:, fw:2 * fw])
    o_ref[...] = (y * scale).astype(BF16)


def _fourier_short(acas, seq):
    n, fw2 = acas.shape
    fw = fw2 // 2
    idx = (np.arange(seq)[:, None] * np.arange(seq)[None, :]) % seq
    ang = (2.0 * np.pi / seq) * idx
    ct = jnp.asarray(np.cos(ang), BF16)
    st = jnp.asarray(np.sin(ang), BF16)
    scale = 1.0 / math.sqrt(seq * FOURIER_GROUP_DIM)
    return pl.pallas_call(
        functools.partial(_fourier_short_kernel, scale=scale),
        out_shape=jax.ShapeDtypeStruct((n, fw), BF16),
        grid=(n // seq,),
        in_specs=[_const_spec((seq, seq)), _const_spec((seq, seq)),
                  pl.BlockSpec((seq, fw2), lambda b: (b, 0))],
        out_specs=pl.BlockSpec((seq, fw), lambda b: (b, 0)),
        compiler_params=_cparams(("parallel",)),
        name="fourier_ctx",
    )(ct, st, acas)


def _fourier_long_kernel(cb_ref, sb_ref, ca_ref, sa_ref, acas_ref, o_ref, ct_scr, st_scr, *, scale):
    @pl.when(pl.program_id(1) == 0)
    def _():
        tb, seq = cb_ref.shape
        cw = 512
        for part in range(ca_ref.shape[0]):
            rows = slice(part * tb, (part + 1) * tb)
            for c0 in range(0, seq, cw):
                ca, sa = ca_ref[part, :, c0:c0 + cw], sa_ref[part, :, c0:c0 + cw]
                cb, sb = cb_ref[:, c0:c0 + cw], sb_ref[:, c0:c0 + cw]
                ct_scr[rows, c0:c0 + cw] = (ca * cb - sa * sb).astype(BF16)
                st_scr[rows, c0:c0 + cw] = (sa * cb + ca * sb).astype(BF16)
    fw = o_ref.shape[1]
    y = _dot(ct_scr[...], acas_ref[:, 0:fw]) - _dot(st_scr[...], acas_ref[:, fw:2 * fw])
    o_ref[...] = (y * scale).astype(BF16)


def _fourier_long(acas, seq):
    n, fw2 = acas.shape
    fw = fw2 // 2
    nb = n // seq
    tr = min(FOURIER_ROW_TILE, seq)
    tb = min(FOURIER_TABLE_ROWS, tr)
    parts = tr // tb
    nr = seq // tr
    col = np.arange(seq)[None, :]
    beta = (2.0 * np.pi / seq) * ((np.arange(tb)[:, None] * col) % seq)
    alpha = (2.0 * np.pi / seq) * (((np.arange(seq // tb) * tb)[:, None] * col) % seq)
    cb, sb = jnp.asarray(np.cos(beta), F32), jnp.asarray(np.sin(beta), F32)
    ca = jnp.asarray(np.cos(alpha), F32).reshape(seq // tb, 1, seq)
    sa = jnp.asarray(np.sin(alpha), F32).reshape(seq // tb, 1, seq)
    scale = 1.0 / math.sqrt(seq * FOURIER_GROUP_DIM)
    return pl.pallas_call(
        functools.partial(_fourier_long_kernel, scale=scale),
        out_shape=jax.ShapeDtypeStruct((n, fw), BF16),
        grid=(nr, nb),
        in_specs=[_const_spec((tb, seq)), _const_spec((tb, seq)),
                  pl.BlockSpec((parts, 1, seq), lambda r, b: (r, 0, 0)),
                  pl.BlockSpec((parts, 1, seq), lambda r, b: (r, 0, 0)),
                  pl.BlockSpec((seq, fw2), lambda r, b: (b, 0))],
        out_specs=pl.BlockSpec((tr, fw), lambda r, b: (b * nr + r, 0)),
        scratch_shapes=[pltpu.VMEM((tr, seq), BF16), pltpu.VMEM((tr, seq), BF16)],
        compiler_params=_cparams(("arbitrary", "arbitrary")),
        name="fourier_lat",
    )(cb, sb, ca, sa, acas)


def _col_reduce(x, op):
    r = x.reshape(x.shape[0] // SUBLANES, SUBLANES, x.shape[1])
    n = r.shape[0]
    while n > 1:
        n //= 2
        r = op(r[:n], r[n:2 * n])
    r = r[0]
    for shift in (4, 2, 1):
        r = op(r, pltpu.roll(r, shift, 0))
    return r[0:1]


def _attn_kernel(*refs, n_chunks, tk, has_ctx, lam_init):
    lam_ref, sub_ref, q_ref, k_ref, v_ref = refs[:5]
    pos = 5
    if has_ctx:
        kc_ref, vc_ref = refs[5:7]
        pos = 7
    o_ref, vt_scr, acc_scr, s_scr = refs[pos:pos + 4]
    if has_ctx:
        vtc_scr = refs[pos + 4]
    tq, hd2 = q_ref.shape
    half = hd2 // 2
    depth = s_scr.shape[0]

    @pl.when(pl.program_id(2) == 0)
    def _():
        for c in range(n_chunks):
            vt_scr[c] = v_ref[c * tk:(c + 1) * tk, :].astype(F32).T.astype(BF16)
        if has_ctx:
            vtc_scr[...] = vc_ref[...].astype(F32).T.astype(BF16)

    qt = q_ref[...].astype(F32).T
    row = lax.broadcasted_iota(jnp.int32, qt.shape, 0)
    qts = (jnp.where(row < half, qt, 0.0).astype(BF16),
           jnp.where(row >= half, qt, 0.0).astype(BF16))

    chunks = [(k_ref, c * tk, tk, functools.partial(vt_scr.__getitem__, c)) for c in range(n_chunks)]
    if has_ctx:
        chunks.append((kc_ref, 0, kc_ref.shape[0], functools.partial(vtc_scr.__getitem__, Ellipsis)))

    chunk_max = {}

    def scores(idx):
        k_src, start, size, _ = chunks[idx]
        kc = k_src[start:start + size, :]
        for j in range(2):
            s = _dot(kc, qts[j])
            s_scr[idx % depth, j, 0:size, :] = s
            chunk_max[idx, j] = _col_reduce(s, jnp.maximum)

    def softmax_pv(idx, state):
        _, _, size, vt = chunks[idx]
        new = []
        for j in range(2):
            s = s_scr[idx % depth, j, 0:size, :]
            m_blk = chunk_max[idx, j]
            if state is None:
                m_new = m_blk
                p = jnp.exp2(s - m_new)
                l_new = _col_reduce(p, jnp.add)
                acc_scr[j] = _dot(vt(), p.astype(BF16))
            else:
                m, l = state[j]
                m_new = jnp.maximum(m, m_blk)
                alpha = jnp.exp2(m - m_new)
                p = jnp.exp2(s - m_new)
                l_new = alpha * l + _col_reduce(p, jnp.add)
                acc_scr[j] = alpha * acc_scr[j] + _dot(vt(), p.astype(BF16))
            new.append((m_new, l_new))
        return tuple(new)

    state = None
    ahead = depth - 1
    for idx in range(min(ahead, len(chunks))):
        scores(idx)
    for idx in range(len(chunks)):
        if idx + ahead < len(chunks):
            scores(idx + ahead)
        state = softmax_pv(idx, state)
    (_, l1), (_, l2) = state
    lp = lam_ref[...]
    lam = (jnp.exp(jnp.sum(lp[0:1] * lp[1:2], axis=-1, keepdims=True))
           - jnp.exp(jnp.sum(lp[2:3] * lp[3:4], axis=-1, keepdims=True)) + lam_init)
    ot = acc_scr[0] * (1.0 / l1) - lam * (acc_scr[1] * (1.0 / l2))
    o = ot.T
    o = o * lax.rsqrt(jnp.mean(o * o, axis=-1, keepdims=True) + SUBLN_EPS)
    o_ref[...] = (o * sub_ref[...] * (1.0 - lam_init)).astype(BF16)


def _attention(lam_p, sub_g, q, k, v, k_ctx, v_ctx, *, seq, lam_init):
    n, dw = q.shape
    nb = n // seq
    hd2 = dw // N_HEADS
    tq = min(ATTN_Q_TILE, seq)
    tk = min(ATTN_K_TILE, seq)
    nq = seq // tq
    has_ctx = k_ctx is not None
    in_specs = [_const_spec(lam_p.shape), _const_spec(sub_g.shape),
                pl.BlockSpec((tq, hd2), lambda b, h, i: (b * nq + i, h)),
                pl.BlockSpec((seq, hd2), lambda b, h, i: (b, h)),
                pl.BlockSpec((seq, hd2), lambda b, h, i: (b, h))]
    args = [lam_p, sub_g, q, k, v]
    scratch = [pltpu.VMEM((seq // tk, hd2, tk), BF16), pltpu.VMEM((2, hd2, tq), F32),
               pltpu.VMEM((ATTN_SCORE_BUFFERS, 2, tk, tq), F32)]
    if has_ctx:
        lc = k_ctx.shape[0] // nb
        in_specs += [pl.BlockSpec((lc, hd2), lambda b, h, i: (b, h))] * 2
        args += [k_ctx, v_ctx]
        scratch.append(pltpu.VMEM((hd2, lc), BF16))
    return pl.pallas_call(
        functools.partial(_attn_kernel, n_chunks=seq // tk, tk=tk, has_ctx=has_ctx, lam_init=lam_init),
        out_shape=jax.ShapeDtypeStruct((n, dw), BF16),
        grid=(nb, N_HEADS, nq),
        in_specs=in_specs,
        out_specs=pl.BlockSpec((tq, hd2), lambda b, h, i: (b * nq + i, h)),
        scratch_shapes=scratch,
        compiler_params=_cparams(("parallel", "parallel", "arbitrary")),
        name="attn_lat" if has_ctx else "attn_ctx",
    )(*args)


def _outproj_ffn_kernel(xc_ref, fc_ref, ac_ref, xl_ref, fl_ref, al_ref, wo_ref, mod_ref, g_ref, w1_ref, w3_ref,
                        w2_ref, o_ref, x_scr, *, ctx_tiles):
    m = mod_ref[0]

    def residual(x_ref, four_ref, attn_ref):
        fw = four_ref.shape[1]
        y = _dot(four_ref[...], wo_ref[0:fw, :]) + _dot(attn_ref[...], wo_ref[fw:, :])
        x_scr[...] = x_ref[...] + m[2:3] * y

    @pl.when(pl.program_id(0) < ctx_tiles)
    def _():
        residual(xc_ref, fc_ref, ac_ref)

    @pl.when(pl.program_id(0) >= ctx_tiles)
    def _():
        residual(xl_ref, fl_ref, al_ref)

    x = x_scr[...]
    h = _norm_mod(x, g_ref[...], m[3:4], m[4:5]).astype(BF16)
    a = (_silu(_dot(h, w1_ref[...])) * _dot(h, w3_ref[...])).astype(BF16)
    o_ref[...] = x + m[5:6] * _dot(a, w2_ref[...])


def _outproj_ffn(ctx, lat, w_out, mod, g, w1, w3, w2, *, seq):
    n_ctx, d = ctx[0].shape
    n_lat = lat[0].shape[0]
    tm = TOKEN_TILE
    tc = n_ctx // tm
    ctx_row = lambda i: (jnp.minimum(i, tc - 1), 0)
    lat_row = lambda i: (jnp.maximum(i - tc, 0), 0)
    in_specs = ([pl.BlockSpec((tm, a.shape[1]), ctx_row) for a in ctx]
                + [pl.BlockSpec((tm, a.shape[1]), lat_row) for a in lat]
                + [_const_spec(w_out.shape), pl.BlockSpec((1, 6, d), _seg_index_map(tm, n_ctx, seq)),
                   _const_spec((1, d)), _const_spec(w1.shape), _const_spec(w3.shape), _const_spec(w2.shape)])
    return pl.pallas_call(
        functools.partial(_outproj_ffn_kernel, ctx_tiles=tc),
        out_shape=jax.ShapeDtypeStruct((n_ctx + n_lat, d), F32),
        grid=((n_ctx + n_lat) // tm,), in_specs=in_specs,
        out_specs=pl.BlockSpec((tm, d), lambda i: (i, 0)),
        scratch_shapes=[pltpu.VMEM((tm, d), F32)],
        compiler_params=_cparams(("parallel",)),
        name="outproj_ffn",
    )(*ctx, *lat, w_out, mod, g, w1, w3, w2)


def _conv_kernel(x_ref, prev_ref, next_ref, g_ref, w1_ref, wdw_ref, bdw_ref, lng_ref, lnb_ref, w2_ref, mod_ref,
                 o_ref, win_scr, conv_scr, shift_scr, *, n_ctx, seq_ctx, seq_lat, taps, front):
    tm, d = x_ref.shape
    start = pl.program_id(0) * tm
    is_ctx = start < n_ctx
    off = jnp.where(is_ctx, start % seq_ctx, (start - n_ctx) % seq_lat)
    slen = jnp.where(is_ctx, seq_ctx, seq_lat)
    m = mod_ref[0]
    h = jnp.concatenate([_norm_mod(r[...], g_ref[...], m[0:1], m[1:2]).astype(BF16)
                         for r in (prev_ref, x_ref, next_ref)], axis=0)
    for cb in range(d // LANES):
        vg = _dot(h, w1_ref[:, 2 * cb * LANES:2 * (cb + 1) * LANES])
        u = vg[:, 0:LANES] * jax.nn.sigmoid(vg[:, LANES:2 * LANES])
        cols = slice(cb * LANES, (cb + 1) * LANES)
        win_scr[0:CONV_HALO, cols] = jnp.where(off == 0, 0.0, u[0:CONV_HALO])
        win_scr[CONV_HALO:CONV_HALO + tm, cols] = u[CONV_HALO:CONV_HALO + tm]
        win_scr[CONV_HALO + tm:, cols] = jnp.where(off + tm == slen, 0.0, u[CONV_HALO + tm:])
    n_q = wdw_ref.shape[0]
    rc = CONV_ROW_CHUNK
    for cb in range(d // LANES):
        cols = slice(cb * LANES, (cb + 1) * LANES)
        for k in range(1, SUBLANES):
            shift_scr[k] = win_scr[k:k + tm + n_q - SUBLANES, cols]
        w = {q: wdw_ref[q, :, cols] for q in range(front, front + taps)}
        bias = bdw_ref[:, cols]
        for r0 in range(0, tm, rc):
            acc = None
            for q in range(front, front + taps):
                k = q % SUBLANES
                if k:
                    tap = shift_scr[k, r0 + q - k:r0 + q - k + rc, :]
                else:
                    tap = win_scr[r0 + q:r0 + q + rc, cols]
                term = tap.reshape(rc // SUBLANES, SUBLANES, LANES) * w[q][None]
                acc = term if acc is None else acc + term
            conv_scr[r0:r0 + rc, cols] = acc.reshape(rc, LANES) + bias
    c = conv_scr[...]
    mu = jnp.mean(c, axis=-1, keepdims=True)
    cc = c - mu
    var = jnp.mean(cc * cc, axis=-1, keepdims=True)
    y = _silu(cc * lax.rsqrt(var + NORM_EPS) * lng_ref[...] + lnb_ref[...]).astype(BF16)
    o_ref[...] = x_ref[...] + m[2:3] * _dot(y, w2_ref[...])


def _conv(x_all, mod, g, w_pw1, w_dw, b_dw, ln_g, ln_b, w_pw2, *, n_ctx, seq_ctx, seq_lat):
    n, d = x_all.shape
    tm = min(CONV_TILE, seq_ctx)
    w_pw1 = w_pw1.reshape(d, 2, d // LANES, LANES).transpose(0, 2, 1, 3).reshape(d, 2 * d)
    taps = w_dw.shape[0]
    front = CONV_HALO - taps // 2
    w_dw = jnp.pad(w_dw, ((front, 2 * CONV_HALO - taps - front), (0, 0)))
    w_dw = jnp.broadcast_to(w_dw[:, None, :], (2 * CONV_HALO, SUBLANES, d))
    hb = tm // CONV_HALO
    last = n // CONV_HALO - 1
    row = lambda i: (i, 0)
    return pl.pallas_call(
        functools.partial(_conv_kernel, n_ctx=n_ctx, seq_ctx=seq_ctx, seq_lat=seq_lat, taps=taps, front=front),
        out_shape=jax.ShapeDtypeStruct((n, d), F32),
        grid=(n // tm,),
        in_specs=[pl.BlockSpec((tm, d), row),
                  pl.BlockSpec((CONV_HALO, d), lambda i: (jnp.maximum(i * hb - 1, 0), 0)),
                  pl.BlockSpec((CONV_HALO, d), lambda i: (jnp.minimum((i + 1) * hb, last), 0)),
                  _const_spec((1, d)), _const_spec(w_pw1.shape),
                  _const_spec(w_dw.shape), _const_spec((1, d)), _const_spec((1, d)), _const_spec((1, d)),
                  _const_spec(w_pw2.shape),
                  pl.BlockSpec((1, 6, d), _seg_index_map(tm, n_ctx, seq_lat))],
        out_specs=pl.BlockSpec((tm, d), row),
        scratch_shapes=[pltpu.VMEM((tm + 2 * CONV_HALO, d), F32), pltpu.VMEM((tm, d), F32),
                        pltpu.VMEM((SUBLANES, tm + 2 * CONV_HALO - SUBLANES, LANES), F32)],
        compiler_params=_cparams(("parallel",)),
        name="conv_module",
    )(x_all, x_all, x_all, g, w_pw1, w_dw, b_dw, ln_g, ln_b, w_pw2, mod)


def _router_kernel(x_ref, mod_ref, g_ref, wr_ref, tri_ref, upper_ref, h_ref, info_ref, runs_ref, *, n_experts):
    m = mod_ref[0]
    h = _norm_mod(x_ref[...], g_ref[...], m[3:4], m[4:5])
    h_hi = h.astype(BF16)
    h_ref[...] = h_hi
    h_lo = (h - h_hi.astype(F32)).astype(BF16)
    wr = wr_ref[...]
    w_hi = wr.astype(BF16)
    w_lo = (wr - w_hi.astype(F32)).astype(BF16)
    logits = _dot(h_hi, w_hi) + (_dot(h_lo, w_hi) + _dot(h_hi, w_lo))
    lane = lax.broadcasted_iota(jnp.int32, logits.shape, 1)
    neg = jnp.float32(-jnp.inf)
    logits = jnp.where(lane < n_experts, logits, neg)
    m1 = jnp.max(logits, axis=-1, keepdims=True)
    i1 = jnp.min(jnp.where(logits == m1, lane, LANES), axis=-1, keepdims=True)
    oh1 = lane == i1
    rest = jnp.where(oh1, neg, logits)
    m2 = jnp.max(rest, axis=-1, keepdims=True)
    i2 = jnp.min(jnp.where(rest == m2, lane, LANES), axis=-1, keepdims=True)
    oh2 = lane == i2
    e = jnp.exp(m2 - m1)
    g1 = 1.0 / (1.0 + e)
    g2 = e * g1
    oh = jnp.where(oh1, 1.0, 0.0) + jnp.where(oh2, 1.0, 0.0)
    cnt = jnp.sum(oh, axis=0, keepdims=True)
    run_units = jnp.floor((cnt + (RUN_ALIGN - 1.0)) * (1.0 / RUN_ALIGN))
    units8 = jnp.broadcast_to(run_units, (SUBLANES, LANES)).astype(BF16)
    run_start = _dot(units8, upper_ref[...])[0:1, :] * RUN_ALIGN
    before = _dot(tri_ref[...], oh.astype(BF16)) + run_start
    p1 = jnp.sum(jnp.where(oh1, before, 0.0), axis=-1, keepdims=True)
    p2 = jnp.sum(jnp.where(oh2, before, 0.0), axis=-1, keepdims=True)
    info = jnp.where(lane == 0, p1,
           jnp.where(lane == 1, p2,
           jnp.where(lane == 2, g1,
           jnp.where(lane == 3, g2, 0.0))))
    info_ref[...] = info
    sub = lax.broadcasted_iota(jnp.int32, (SUBLANES, LANES), 0)
    runs_ref[0] = jnp.where(sub == 0, run_units * RUN_ALIGN, jnp.where(sub == 1, run_start, 0.0))


def _router(x_all, mod, g, w_router, *, n_ctx, seq):
    n, d = x_all.shape
    n_experts = w_router.shape[1]
    tm = TOKEN_TILE
    wr = jnp.pad(w_router, ((0, 0), (0, LANES - n_experts)))
    tri = jnp.asarray(np.tril(np.ones((tm, tm), np.float32), -1), BF16)
    upper = jnp.asarray(np.triu(np.ones((LANES, LANES), np.float32), 1), BF16)
    row = lambda i: (i, 0)
    return pl.pallas_call(
        functools.partial(_router_kernel, n_experts=n_experts),
        out_shape=[jax.ShapeDtypeStruct((n, d), BF16), jax.ShapeDtypeStruct((n, LANES), F32),
                   jax.ShapeDtypeStruct((n // tm, SUBLANES, LANES), F32)],
        grid=(n // tm,),
        in_specs=[pl.BlockSpec((tm, d), row), pl.BlockSpec((1, 6, d), _seg_index_map(tm, n_ctx, seq)),
                  _const_spec((1, d)), _const_spec(wr.shape), _const_spec(tri.shape), _const_spec(upper.shape)],
        out_specs=[pl.BlockSpec((tm, d), row), pl.BlockSpec((tm, LANES), row),
                   pl.BlockSpec((1, SUBLANES, LANES), lambda i: (i, 0, 0))],
        compiler_params=_cparams(("parallel",)),
        name="moe_router",
    )(x_all, mod, g, wr, tri, upper)


def _run_copies(n_experts, len_ref, start_ref, dst_ref, tile, make_copy):
    ops = []
    for e in range(n_experts):
        n = len_ref[tile * n_experts + e]
        src = start_ref[tile * n_experts + e]
        dst = dst_ref[tile * n_experts + e]
        ops.append((n > 0, make_copy(pl.multiple_of(src, RUN_ALIGN), pl.multiple_of(dst, RUN_ALIGN),
                                     pl.multiple_of(n, RUN_ALIGN))))
    return ops


def _dispatch_kernel(zrow_ref, len_ref, start_ref, dst_ref, h_ref, info_ref, xg_ref, zero_scr, xs_scr, sem, zsem,
                     *, n_zero, n_experts):
    tm = h_ref.shape[0]
    tz = zero_scr.shape[0]
    rows = xs_scr.shape[1]

    def zero_copy(e):
        return pltpu.make_async_copy(zero_scr, xg_ref.at[pl.ds(pl.multiple_of(zrow_ref[e], tz), tz)], zsem)

    @pl.when(pl.program_id(0) == 0)
    def _():
        zero_scr[...] = jnp.zeros_like(zero_scr)
        for e in range(n_zero):
            @pl.when(zrow_ref[e] >= 0)
            def _():
                zero_copy(e).start()
        for e in range(n_zero):
            @pl.when(zrow_ref[e] >= 0)
            def _():
                zero_copy(e).wait()

    info_t = info_ref[...].T
    slot1 = info_t[0:1, :].astype(jnp.int32)
    slot2 = info_t[1:2, :].astype(jnp.int32)
    r = lax.broadcasted_iota(jnp.int32, (rows, tm), 0)
    perm = (jnp.where(r == slot1, 1.0, 0.0) + jnp.where(r == slot2, 1.0, 0.0)).astype(BF16)
    step = pl.program_id(0)
    buf = step % 2
    xs_scr[buf] = _dot(perm, h_ref[...])

    def run_writes(tile, b):
        def make_copy(src, dst, size):
            return pltpu.make_async_copy(xs_scr.at[b, pl.ds(src, size)], xg_ref.at[pl.ds(dst, size)], sem.at[b])
        return _run_copies(n_experts, len_ref, start_ref, dst_ref, tile, make_copy)

    for cond, copy in run_writes(step, buf):
        pl.when(cond)(copy.start)

    @pl.when(step > 0)
    def _():
        for cond, copy in run_writes(step - 1, 1 - buf):
            pl.when(cond)(copy.wait)

    @pl.when(step == pl.num_programs(0) - 1)
    def _():
        for cond, copy in run_writes(step, buf):
            pl.when(cond)(copy.wait)


def _dispatch(h_all, info, run_len, run_start, run_dst, zero_tile_row, n_rows):
    n, d = h_all.shape
    tm = TOKEN_TILE
    n_zero = zero_tile_row.shape[0]
    n_experts = run_len.shape[0] // (n // tm)
    grid_spec = pltpu.PrefetchScalarGridSpec(
        num_scalar_prefetch=4,
        grid=(n // tm,),
        in_specs=[pl.BlockSpec((tm, d), lambda i, *_: (i, 0)),
                  pl.BlockSpec((tm, LANES), lambda i, *_: (i, 0))],
        out_specs=pl.BlockSpec(memory_space=pl.ANY),
        scratch_shapes=[pltpu.VMEM((MOE_TILE, d), F32), pltpu.VMEM((2, _local_rows(tm, n_experts), d), F32),
                        pltpu.SemaphoreType.DMA((2,)), pltpu.SemaphoreType.DMA(())],
    )
    return pl.pallas_call(
        functools.partial(_dispatch_kernel, n_zero=n_zero, n_experts=n_experts),
        out_shape=jax.ShapeDtypeStruct((n_rows, d), F32),
        grid_spec=grid_spec,
        compiler_params=_cparams(("arbitrary",)),
        name="moe_dispatch",
    )(zero_tile_row, run_len, run_start, run_dst, h_all, info)


def _experts_kernel(te_ref, nt_ref, x_ref, w1_ref, w3_ref, w2_ref, o_ref, xb_scr):
    c = pl.program_id(1)

    @pl.when(pl.program_id(0) < nt_ref[0])
    def _():
        @pl.when(c == 0)
        def _():
            xb_scr[...] = x_ref[...].astype(BF16)
        xb = xb_scr[...]
        a = (_silu(_dot(xb, w1_ref[0])) * _dot(xb, w3_ref[0])).astype(BF16)
        y = _dot(a, w2_ref[0])

        @pl.when(c == 0)
        def _():
            o_ref[...] = y

        @pl.when(c > 0)
        def _():
            o_ref[...] += y

    @pl.when(jnp.logical_and(pl.program_id(0) >= nt_ref[0], c == 0))
    def _():
        o_ref[...] = jnp.zeros_like(o_ref)


def _experts(xg, w1, w3, w2, tile_expert, n_tiles_used):
    p, d = xg.shape
    tm = MOE_TILE
    ff = w1.shape[2]
    fc = MOE_FF_CHUNK if ff % MOE_FF_CHUNK == 0 else ff
    nc = ff // fc
    n_tiles = p // tm
    wmode = {}

    def tile(t, nt):
        return jnp.minimum(t, nt[0] - 1)

    def chunk(t, c, nt):
        return jnp.where(t < nt[0], c, nc - 1)

    grid_spec = pltpu.PrefetchScalarGridSpec(
        num_scalar_prefetch=2,
        grid=(n_tiles, nc),
        in_specs=[pl.BlockSpec((tm, d), lambda t, c, te, nt: (tile(t, nt), 0)),
                  pl.BlockSpec((1, d, fc), lambda t, c, te, nt: (te[tile(t, nt)], 0, chunk(t, c, nt)), **wmode),
                  pl.BlockSpec((1, d, fc), lambda t, c, te, nt: (te[tile(t, nt)], 0, chunk(t, c, nt)), **wmode),
                  pl.BlockSpec((1, fc, d), lambda t, c, te, nt: (te[tile(t, nt)], chunk(t, c, nt), 0), **wmode)],
        out_specs=pl.BlockSpec((tm, d), lambda t, c, te, nt: (t, 0)),
        scratch_shapes=[pltpu.VMEM((tm, d), BF16)],
    )
    return pl.pallas_call(
        _experts_kernel,
        out_shape=jax.ShapeDtypeStruct((p, d), F32),
        grid_spec=grid_spec,
        compiler_params=_cparams(("arbitrary", "arbitrary"), vmem=MOE_VMEM_LIMIT),
        name="moe_experts",
    )(tile_expert, n_tiles_used, xg, w1, w3, w2)


def _combine_kernel(len_ref, start_ref, dst_ref, yg_ref, x_ref, info_ref, mod_ref, fg_ref, o_ref, ys_scr, sem,
                    *, n_experts, tile0):
    tm = x_ref.shape[0]
    rows = ys_scr.shape[1]
    step = pl.program_id(0)
    buf = step % 2

    def run_reads(tile, b):
        def make_copy(src, dst, size):
            return pltpu.make_async_copy(yg_ref.at[pl.ds(dst, size)], ys_scr.at[b, pl.ds(src, size)], sem.at[b])
        return _run_copies(n_experts, len_ref, start_ref, dst_ref, tile0 + tile, make_copy)

    @pl.when(step == 0)
    def _():
        ys_scr[...] = jnp.zeros_like(ys_scr)
        for cond, copy in run_reads(step, buf):
            pl.when(cond)(copy.start)

    @pl.when(step + 1 < pl.num_programs(0))
    def _():
        for cond, copy in run_reads(step + 1, 1 - buf):
            pl.when(cond)(copy.start)

    for cond, copy in run_reads(step, buf):
        pl.when(cond)(copy.wait)
    info = info_ref[...]
    ys = ys_scr[buf].astype(BF16)
    lane = lax.broadcasted_iota(jnp.int32, (tm, rows), 1)
    pick1 = jnp.where(lane == info[:, 0:1].astype(jnp.int32), 1.0, 0.0).astype(BF16)
    pick2 = jnp.where(lane == info[:, 1:2].astype(jnp.int32), 1.0, 0.0).astype(BF16)
    y = info[:, 2:3] * _dot(pick1, ys) + info[:, 3:4] * _dot(pick2, ys)
    x = x_ref[...] + mod_ref[0][5:6] * y
    o_ref[...] = x * lax.rsqrt(jnp.mean(x * x, axis=-1, keepdims=True) + NORM_EPS) * fg_ref[...]


def _combine(yg, x_all, info, run_len, run_start, run_dst, mod, final_g, *, row0, n_rows, seq, mod_row0):
    n, d = x_all.shape
    tm = TOKEN_TILE
    t0 = row0 // tm
    per_seq = seq // tm
    n_experts = run_len.shape[0] // (n // tm)
    grid_spec = pltpu.PrefetchScalarGridSpec(
        num_scalar_prefetch=3,
        grid=(n_rows // tm,),
        in_specs=[pl.BlockSpec(memory_space=pl.ANY),
                  pl.BlockSpec((tm, d), lambda i, *_: (t0 + i, 0)),
                  pl.BlockSpec((tm, LANES), lambda i, *_: (t0 + i, 0)),
                  pl.BlockSpec((1, 6, d), lambda i, *_: (mod_row0 + (i // per_seq if mod_row0 else 0), 0, 0)),
                  pl.BlockSpec((1, d), lambda i, *_: (0, 0))],
        out_specs=pl.BlockSpec((tm, d), lambda i, *_: (i, 0)),
        scratch_shapes=[pltpu.VMEM((2, _local_rows(tm, n_experts), d), F32), pltpu.SemaphoreType.DMA((2,))],
    )
    return pl.pallas_call(
        functools.partial(_combine_kernel, n_experts=n_experts, tile0=t0),
        out_shape=jax.ShapeDtypeStruct((n_rows, d), F32),
        grid_spec=grid_spec,
        compiler_params=_cparams(("arbitrary",)),
        name="moe_combine",
    )(run_len, run_start, run_dst, yg, x_all, info, mod, final_g)


def _moe(x_all, mod, g, w_router, w1, w3, w2, final_g, *, n_ctx, seq_ctx_total, seq_lat):
    n, d = x_all.shape
    n_experts = w_router.shape[1]
    tm = TOKEN_TILE
    h_all, info, runs = _router(x_all, mod, g, w_router, n_ctx=n_ctx, seq=seq_lat)
    run_len = runs[:, 0, :n_experts].astype(jnp.int32)
    run_start = runs[:, 1, :n_experts].astype(jnp.int32)
    cnt = jnp.sum(run_len, axis=0)
    padded = ((cnt + MOE_TILE - 1) // MOE_TILE) * MOE_TILE
    ends = jnp.cumsum(padded)
    offs = ends - padded
    run_dst = offs[None, :] + jnp.cumsum(run_len, axis=0) - run_len
    n_tiles = -(-(TOP_K * n + (n // tm) * n_experts * (RUN_ALIGN - 1)) // MOE_TILE) + n_experts
    n_used = (ends[-1] // MOE_TILE).astype(jnp.int32)
    tile_start = jnp.arange(n_tiles, dtype=jnp.int32) * MOE_TILE
    tile_expert = jnp.minimum(jnp.sum(tile_start[:, None] >= ends[None, :], axis=1), n_experts - 1).astype(jnp.int32)
    tile_expert = jnp.where(tile_start < ends[-1], tile_expert, tile_expert[jnp.maximum(n_used - 1, 0)])
    tail_start = tile_start[(TOP_K * n) // MOE_TILE:]
    zero_tile_row = jnp.concatenate([jnp.where(padded > 0, ends - MOE_TILE, -1),
                                     jnp.where(tail_start >= ends[-1], tail_start, -1)]).astype(jnp.int32)
    run_args = (run_len.reshape(-1), run_start.reshape(-1), run_dst.reshape(-1).astype(jnp.int32))
    xg = _dispatch(h_all, info, *run_args, zero_tile_row, n_tiles * MOE_TILE)
    yg = _experts(xg, w1, w3, w2, tile_expert, n_used.reshape(1))
    y_ctx = _combine(yg, x_all, info, *run_args, mod, final_g, row0=0, n_rows=n_ctx, seq=seq_ctx_total, mod_row0=0)
    y_lat = _combine(yg, x_all, info, *run_args, mod, final_g, row0=n_ctx, n_rows=n - n_ctx, seq=seq_lat,
                     mod_row0=1)
    return y_ctx, y_lat


def _rope_tables(seq, hd):
    axis_dim = hd // 2
    nf = axis_dim // 2
    inv = jnp.power(ROPE_BASE, -jnp.arange(nf, dtype=F32) / nf)
    pos = np.arange(seq)
    row = jnp.asarray(pos // GRID_W, F32)
    col = jnp.asarray(pos % GRID_W, F32)
    lane = np.arange(2 * hd)
    dd = lane % hd
    use_col = (dd // axis_dim) == 1
    e = dd % axis_dim
    first = jnp.asarray(e < nf)[None, :]
    freq = inv[e % nf]
    ang = jnp.where(jnp.asarray(use_col)[None, :], col[:, None], row[:, None]) * freq[None, :]
    cos, sin = jnp.cos(ang), jnp.sin(ang)
    return cos, jnp.where(first, -sin, 0.0), jnp.where(first, 0.0, sin)


def _channel_dft(fw):
    g = FOURIER_GROUP_DIM
    ang = (2.0 * np.pi / g) * ((np.arange(g)[:, None] * np.arange(g)[None, :]) % g)
    eye = np.eye(fw // g)
    return jnp.asarray(np.concatenate([np.kron(eye, np.cos(ang)), np.kron(eye, np.sin(ang))], axis=1), BF16)


def kernel(x_prompt, x_sample, cache_k, cache_v, c, c_ctx, w_ada, b_ada, norm_g, final_g, w_in_ab, w_out_ab,
           diff_lambda, diff_subln, conv_w_pw1, conv_w_dw, conv_b_dw, conv_ln_g, conv_ln_b, conv_w_pw2,
           ffn_w1, ffn_w3, ffn_w2, moe_router, moe_w1, moe_w3, moe_w2):
    bp, lp, d = x_prompt.shape
    bs, ls, _ = x_sample.shape
    lc = cache_k.shape[2]
    n_ctx, n_lat = bp * lp, bs * ls
    n_all = n_ctx + n_lat
    dw = N_HEADS * cache_k.shape[-1]
    hd = cache_k.shape[-1] // 2
    fw = w_in_ab.shape[2] - 3 * dw

    rows = -(-(1 + bs) // SUBLANES) * SUBLANES
    cond = jnp.zeros((rows, d), F32).at[0].set(c_ctx).at[1:1 + bs].set(c)
    mod = _modulation(cond, w_ada, b_ada)

    lam_init = 0.8 - 0.6 * math.exp(-0.3 * 0)
    q_scale = (hd ** -0.5) * LOG2E
    w_in = w_in_ab[0].astype(BF16)
    bd = _channel_dft(fw)
    g00 = norm_g[0, 0].reshape(1, d)
    xp = x_prompt.reshape(n_ctx, d)
    xs = x_sample.reshape(n_lat, d)
    acas_p, q_p, k_p, v_p, kc_new, vc_new = _inproj(
        xp, mod[0], g00, w_in, bd, seq=lp, mod_row0=0, rope_tables=None, cache=True, q_scale=q_scale)
    acas_s, q_s, k_s, v_s = _inproj(
        xs, mod[0], g00, w_in, bd, seq=ls, mod_row0=1, rope_tables=_rope_tables(ls, hd), cache=False, q_scale=q_scale)
    four_p = _fourier_short(acas_p, lp)
    four_s = _fourier_long(acas_s, ls)
    lam_p = diff_lambda[0]
    sub_g = diff_subln[0].reshape(1, 2 * hd)
    attn_p = _attention(lam_p, sub_g, q_p, k_p, v_p, None, None, seq=lp, lam_init=lam_init)
    k_ctx = cache_k[:, 0].reshape(bs * lc, dw).astype(BF16)
    v_ctx = cache_v[:, 0].reshape(bs * lc, dw).astype(BF16)
    attn_s = _attention(lam_p, sub_g, q_s, k_s, v_s, k_ctx, v_ctx, seq=ls, lam_init=lam_init)
    w_out = w_out_ab[0].astype(BF16)
    x_all = _outproj_ffn((xp, four_p, attn_p), (xs, four_s, attn_s), w_out, mod[0], norm_g[0, 1].reshape(1, d),
                         ffn_w1[0].astype(BF16), ffn_w3[0].astype(BF16), ffn_w2[0].astype(BF16), seq=ls)

    x_all = _conv(x_all, mod[1], norm_g[1, 0].reshape(1, d), conv_w_pw1[0].astype(BF16), conv_w_dw[0],
                  conv_b_dw[0].reshape(1, d), conv_ln_g[0].reshape(1, d), conv_ln_b[0].reshape(1, d),
                  conv_w_pw2[0].astype(BF16), n_ctx=n_ctx, seq_ctx=lp, seq_lat=ls)
    y_ctx, y_lat = _moe(x_all, mod[1], norm_g[1, 1].reshape(1, d), moe_router[0], moe_w1[0].astype(BF16),
                        moe_w3[0].astype(BF16), moe_w2[0].astype(BF16), final_g.reshape(1, d),
                        n_ctx=n_ctx, seq_ctx_total=n_ctx, seq_lat=ls)

    n_even = w_in_ab.shape[0]
    new_k = kc_new.reshape(bp, n_even, lp, N_HEADS, 2 * hd)
    new_v = vc_new.reshape(bp, n_even, lp, N_HEADS, 2 * hd)
    return (y_ctx.reshape(bp, lp, d), y_lat.reshape(bs, ls, d), new_k, new_v)
```

```python
import functools
import math

import numpy as np
import jax
import jax.numpy as jnp
from jax import lax
from jax.experimental import pallas as pl
from jax.experimental.pallas import tpu as pltpu

F32 = jnp.float32
BF16 = jnp.bfloat16

GRID_W = 64
ROPE_BASE = 10000.0
N_HEADS = 4
FOURIER_GROUP_DIM = 128
CONV_HALO = 16
TOP_K = 2
NORM_EPS = 1e-6
SUBLN_EPS = 1e-5
LANES = 128
SUBLANES = 8
LOG2E = 1.4426950408889634

TOKEN_TILE = 512
CONV_TILE = 256
CONV_ROW_CHUNK = 64
ATTN_Q_TILE = 512
ATTN_K_TILE = 512
ATTN_SCORE_BUFFERS = 2
FOURIER_ROW_TILE = 512
FOURIER_TABLE_ROWS = 256
MOE_TILE = 512
MOE_FF_CHUNK = 3584
RUN_ALIGN = SUBLANES
VMEM_LIMIT = 56 * 1024 * 1024
MOE_VMEM_LIMIT = 62 * 1024 * 1024


def _local_rows(tile, n_experts):
    return -(-(TOP_K * tile + n_experts * (RUN_ALIGN - 1)) // LANES) * LANES


def _cparams(semantics, vmem=VMEM_LIMIT):
    return pltpu.CompilerParams(dimension_semantics=semantics, vmem_limit_bytes=vmem)


def _const_spec(shape):
    nd = len(shape)
    return pl.BlockSpec(shape, lambda *_: (0,) * nd, pipeline_mode=pl.Buffered(1))


def _dot(a, b):
    return jnp.dot(a, b, preferred_element_type=F32)


def _silu(x):
    return x * jax.nn.sigmoid(x)


def _norm_mod(x, g, shift, scale):
    ms = jnp.mean(x * x, axis=-1, keepdims=True)
    return (x * lax.rsqrt(ms + NORM_EPS)) * (g * (1.0 + scale)) + shift


def _seg_index_map(tile, n_ctx, seq):
    def index_map(i, *_):
        start = i * tile
        return (jnp.where(start < n_ctx, 0, 1 + (start - n_ctx) // seq), 0, 0)
    return index_map


def _mod_kernel(c_ref, w_ref, b_ref, o_ref):
    c = c_ref[...]
    o_ref[0] = _dot(_silu(c).astype(BF16), w_ref[0].astype(BF16)) + b_ref[0]


def _modulation(cond, w_ada, b_ada):
    depth, d, n6 = w_ada.shape
    rows = cond.shape[0]
    tn = 1536
    out = pl.pallas_call(
        _mod_kernel,
        out_shape=jax.ShapeDtypeStruct((depth, rows, n6), F32),
        grid=(depth, n6 // tn),
        in_specs=[pl.BlockSpec((rows, d), lambda l, j: (0, 0)),
                  pl.BlockSpec((1, d, tn), lambda l, j: (l, 0, j)),
                  pl.BlockSpec((1, 1, tn), lambda l, j: (l, 0, j))],
        out_specs=pl.BlockSpec((1, rows, tn), lambda l, j: (l, 0, j)),
        compiler_params=_cparams(("parallel", "parallel")),
        name="modulation",
    )(cond, w_ada, b_ada.reshape(depth, 1, n6))
    return out.reshape(depth, rows, 6, d)


def _rope(t, cos, sin_lo, sin_hi):
    outs = []
    for h in range(t.shape[1] // LANES):
        th = t[:, h * LANES:(h + 1) * LANES]
        outs.append(th * cos + pltpu.roll(th, LANES - 16, 1) * sin_lo + pltpu.roll(th, 16, 1) * sin_hi)
    return jnp.concatenate(outs, axis=-1)


def _inproj_kernel(*refs, rope, cache, q_scale):
    x_ref, mod_ref, g_ref, w_ref, bd_ref = refs[:5]
    pos = 5
    if rope:
        cos_ref, slo_ref, shi_ref = refs[pos:pos + 3]
        pos += 3
    acas_ref, q_ref, k_ref, v_ref = refs[pos:pos + 4]
    m = mod_ref[0]
    h = _norm_mod(x_ref[...], g_ref[...], m[0:1], m[1:2]).astype(BF16)
    fw = bd_ref.shape[0]
    dw = q_ref.shape[1]
    f = _dot(h, w_ref[:, 0:fw])
    acas_ref[...] = _dot(f.astype(BF16), bd_ref[...]).astype(BF16)
    q = _dot(h, w_ref[:, fw:fw + dw])
    k = _dot(h, w_ref[:, fw + dw:fw + 2 * dw])
    v = _dot(h, w_ref[:, fw + 2 * dw:fw + 3 * dw])
    if cache:
        kc_ref, vc_ref = refs[pos + 4:pos + 6]
        kc_ref[...] = k
        vc_ref[...] = v
    if rope:
        cos, slo, shi = cos_ref[...], slo_ref[...], shi_ref[...]
        q = _rope(q, cos, slo, shi)
        k = _rope(k, cos, slo, shi)
    q_ref[...] = (q * q_scale).astype(BF16)
    k_ref[...] = k.astype(BF16)
    v_ref[...] = v.astype(BF16)


def _inproj(x, mod, g, w_in, bd, *, seq, mod_row0, rope_tables, cache, q_scale):
    n, d = x.shape
    tm = min(TOKEN_TILE, seq)
    fw = bd.shape[0]
    dw = (w_in.shape[1] - fw) // 3
    per_seq = seq // tm
    row = lambda i: (i, 0)
    in_specs = [pl.BlockSpec((tm, d), row),
                pl.BlockSpec((1, 6, d), lambda i: (mod_row0 + (i // per_seq if mod_row0 else 0), 0, 0)),
                _const_spec((1, d)), _const_spec(w_in.shape), _const_spec(bd.shape)]
    args = [x, mod, g, w_in, bd]
    if rope_tables is not None:
        in_specs += [pl.BlockSpec((tm, LANES), lambda i: (i % per_seq, 0))] * 3
        args += list(rope_tables)
    out_shape = [jax.ShapeDtypeStruct((n, 2 * fw), BF16)] + [jax.ShapeDtypeStruct((n, dw), BF16)] * 3
    out_specs = [pl.BlockSpec((tm, 2 * fw), row)] + [pl.BlockSpec((tm, dw), row)] * 3
    if cache:
        out_shape += [jax.ShapeDtypeStruct((n, dw), F32)] * 2
        out_specs += [pl.BlockSpec((tm, dw), row)] * 2
    return pl.pallas_call(
        functools.partial(_inproj_kernel, rope=rope_tables is not None, cache=cache, q_scale=q_scale),
        out_shape=out_shape, grid=(n // tm,), in_specs=in_specs, out_specs=out_specs,
        compiler_params=_cparams(("parallel",)),
        name="inproj_ctx" if cache else "inproj_lat",
    )(*args)


def _fourier_short_kernel(ct_ref, st_ref, acas_ref, o_ref, *, scale):
    fw = o_ref.shape[1]
    y = _dot(ct_ref[...], acas_ref[:, 0:fw]) - _dot(st_ref[...], acas_ref[:, fw:2 * fw])
    o_ref[...] = (y * scale).astype(BF16)


def _fourier_short(acas, seq):
    n, fw2 = acas.shape
    fw = fw2 // 2
    idx = (np.arange(seq)[:, None] * np.arange(seq)[None, :]) % seq
    ang = (2.0 * np.pi / seq) * idx
    ct = jnp.asarray(np.cos(ang), BF16)
    st = jnp.asarray(np.sin(ang), BF16)
    scale = 1.0 / math.sqrt(seq * FOURIER_GROUP_DIM)
    return pl.pallas_call(
        functools.partial(_fourier_short_kernel, scale=scale),
        out_shape=jax.ShapeDtypeStruct((n, fw), BF16),
        grid=(n // seq,),
        in_specs=[_const_spec((seq, seq)), _const_spec((seq, seq)),
                  pl.BlockSpec((seq, fw2), lambda b: (b, 0))],
        out_specs=pl.BlockSpec((seq, fw), lambda b: (b, 0)),
        compiler_params=_cparams(("parallel",)),
        name="fourier_ctx",
    )(ct, st, acas)


def _fourier_fold_kernel(flip_ref, acas_ref, o_ref, mid_ref):
    seq, fw2 = acas_ref.shape
    fw = fw2 // 2
    tb = flip_ref.shape[0]
    sign = jnp.where(lax.broadcasted_iota(jnp.int32, (1, fw2), 1) < fw, 1.0, -1.0)
    row = lax.broadcasted_iota(jnp.int32, (tb, fw2), 0)
    for l0 in range(0, seq // 2, tb):
        mirrored = _dot(flip_ref[...], acas_ref[seq - l0 - tb:seq - l0, :])
        if l0:
            mirrored = jnp.where(row == 0, acas_ref[seq - l0:seq - l0 + 1, :].astype(F32), mirrored)
        o_ref[l0:l0 + tb, :] = (acas_ref[l0:l0 + tb, :].astype(F32) + sign * mirrored).astype(BF16)
    mid_ref[0] = jnp.broadcast_to(acas_ref[seq // 2:seq // 2 + 1, 0:fw].astype(F32), (SUBLANES, fw))


def _fourier_long_kernel(cb_ref, sb_ref, ca_ref, sa_ref, fold_ref, mid_ref, o_ref, ct_scr, st_scr, *, scale):
    @pl.when(pl.program_id(1) == 0)
    def _():
        tb, half = cb_ref.shape
        cw = 512
        for part in range(ca_ref.shape[0]):
            rows = slice(part * tb, (part + 1) * tb)
            for c0 in range(0, half, cw):
                ca, sa = ca_ref[part, :, c0:c0 + cw], sa_ref[part, :, c0:c0 + cw]
                cb, sb = cb_ref[:, c0:c0 + cw], sb_ref[:, c0:c0 + cw]
                ct_scr[rows, c0:c0 + cw] = (ca * cb - sa * sb).astype(BF16)
                st_scr[rows, c0:c0 + cw] = (sa * cb + ca * sb).astype(BF16)
    tr, fw = o_ref.shape
    y = _dot(ct_scr[...], fold_ref[:, 0:fw]) - _dot(st_scr[...], fold_ref[:, fw:2 * fw])
    odd = (lax.broadcasted_iota(jnp.int32, (tr, fw), 0) & 1) == 1
    y = y + jnp.where(odd, -mid_ref[0][0:1, :], mid_ref[0][0:1, :])
    o_ref[...] = (y * scale).astype(BF16)


def _fourier_long(acas, seq):
    n, fw2 = acas.shape
    fw = fw2 // 2
    nb = n // seq
    half = seq // 2
    tr = min(FOURIER_ROW_TILE, seq)
    tb = min(FOURIER_TABLE_ROWS, tr)
    parts = tr // tb
    nr = seq // tr
    flip = np.zeros((tb, tb), np.float32)
    flip[np.arange(1, tb), tb - np.arange(1, tb)] = 1.0
    folded, mid = pl.pallas_call(
        _fourier_fold_kernel,
        out_shape=[jax.ShapeDtypeStruct((nb * half, fw2), BF16), jax.ShapeDtypeStruct((nb, SUBLANES, fw), F32)],
        grid=(nb,),
        in_specs=[_const_spec((tb, tb)), pl.BlockSpec((seq, fw2), lambda b: (b, 0))],
        out_specs=[pl.BlockSpec((half, fw2), lambda b: (b, 0)), pl.BlockSpec((1, SUBLANES, fw), lambda b: (b, 0, 0))],
        compiler_params=_cparams(("parallel",)),
        name="fourier_fold",
    )(jnp.asarray(flip, BF16), acas)
    col = np.arange(half)[None, :]
    beta = (2.0 * np.pi / seq) * ((np.arange(tb)[:, None] * col) % seq)
    alpha = (2.0 * np.pi / seq) * (((np.arange(seq // tb) * tb)[:, None] * col) % seq)
    cb, sb = jnp.asarray(np.cos(beta), F32), jnp.asarray(np.sin(beta), F32)
    ca = jnp.asarray(np.cos(alpha), F32).reshape(seq // tb, 1, half)
    sa = jnp.asarray(np.sin(alpha), F32).reshape(seq // tb, 1, half)
    scale = 1.0 / math.sqrt(seq * FOURIER_GROUP_DIM)
    return pl.pallas_call(
        functools.partial(_fourier_long_kernel, scale=scale),
        out_shape=jax.ShapeDtypeStruct((n, fw), BF16),
        grid=(nr, nb),
        in_specs=[_const_spec((tb, half)), _const_spec((tb, half)),
                  pl.BlockSpec((parts, 1, half), lambda r, b: (r, 0, 0)),
                  pl.BlockSpec((parts, 1, half), lambda r, b: (r, 0, 0)),
                  pl.BlockSpec((half, fw2), lambda r, b: (b, 0)),
                  pl.BlockSpec((1, SUBLANES, fw), lambda r, b: (b, 0, 0))],
        out_specs=pl.BlockSpec((tr, fw), lambda r, b: (b * nr + r, 0)),
        scratch_shapes=[pltpu.VMEM((tr, half), BF16), pltpu.VMEM((tr, half), BF16)],
        compiler_params=_cparams(("arbitrary", "arbitrary")),
        name="fourier_lat",
    )(cb, sb, ca, sa, folded, mid)


def _col_reduce(x, op):
    r = x.reshape(x.shape[0] // SUBLANES, SUBLANES, x.shape[1])
    n = r.shape[0]
    while n > 1:
        n //= 2
        r = op(r[:n], r[n:2 * n])
    r = r[0]
    for shift in (4, 2, 1):
        r = op(r, pltpu.roll(r, shift, 0))
    return r[0:1]


def _attn_kernel(*refs, n_chunks, tk, has_ctx, lam_init):
    lam_ref, sub_ref, q_ref, k_ref, v_ref = refs[:5]
    pos = 5
    if has_ctx:
        kc_ref, vc_ref = refs[5:7]
        pos = 7
    o_ref, vt_scr, acc_scr, s_scr = refs[pos:pos + 4]
    if has_ctx:
        vtc_scr = refs[pos + 4]
    tq, hd2 = q_ref.shape
    half = hd2 // 2
    depth = s_scr.shape[0]

    @pl.when(pl.program_id(2) == 0)
    def _():
        for c in range(n_chunks):
            vt_scr[c] = v_ref[c * tk:(c + 1) * tk, :].astype(F32).T.astype(BF16)
        if has_ctx:
            vtc_scr[...] = vc_ref[...].astype(F32).T.astype(BF16)

    qt = q_ref[...].astype(F32).T
    row = lax.broadcasted_iota(jnp.int32, qt.shape, 0)
    qts = (jnp.where(row < half, qt, 0.0).astype(BF16),
           jnp.where(row >= half, qt, 0.0).astype(BF16))

    chunks = [(k_ref, c * tk, tk, functools.partial(vt_scr.__getitem__, c)) for c in range(n_chunks)]
    if has_ctx:
        chunks.append((kc_ref, 0, kc_ref.shape[0], functools.partial(vtc_scr.__getitem__, Ellipsis)))

    chunk_max = {}

    def scores(idx):
        k_src, start, size, _ = chunks[idx]
        kc = k_src[start:start + size, :]
        for j in range(2):
            s = _dot(kc, qts[j])
            s_scr[idx % depth, j, 0:size, :] = s
            chunk_max[idx, j] = _col_reduce(s, jnp.maximum)

    def softmax_pv(idx, state):
        _, _, size, vt = chunks[idx]
        new = []
        for j in range(2):
            s = s_scr[idx % depth, j, 0:size, :]
            m_blk = chunk_max[idx, j]
            if state is None:
                m_new = m_blk
                p = jnp.exp2(s - m_new)
                l_new = _col_reduce(p, jnp.add)
                acc_scr[j] = _dot(vt(), p.astype(BF16))
            else:
                m, l = state[j]
                m_new = jnp.maximum(m, m_blk)
                alpha = jnp.exp2(m - m_new)
                p = jnp.exp2(s - m_new)
                l_new = alpha * l + _col_reduce(p, jnp.add)
                acc_scr[j] = alpha * acc_scr[j] + _dot(vt(), p.astype(BF16))
            new.append((m_new, l_new))
        return tuple(new)

    state = None
    ahead = depth - 1
    for idx in range(min(ahead, len(chunks))):
        scores(idx)
    for idx in range(len(chunks)):
        if idx + ahead < len(chunks):
            scores(idx + ahead)
        state = softmax_pv(idx, state)
    (_, l1), (_, l2) = state
    lp = lam_ref[...]
    lam = (jnp.exp(jnp.sum(lp[0:1] * lp[1:2], axis=-1, keepdims=True))
           - jnp.exp(jnp.sum(lp[2:3] * lp[3:4], axis=-1, keepdims=True)) + lam_init)
    ot = acc_scr[0] * (1.0 / l1) - lam * (acc_scr[1] * (1.0 / l2))
    o = ot.T
    o = o * lax.rsqrt(jnp.mean(o * o, axis=-1, keepdims=True) + SUBLN_EPS)
    o_ref[...] = (o * sub_ref[...] * (1.0 - lam_init)).astype(BF16)


def _attention(lam_p, sub_g, q, k, v, k_ctx, v_ctx, *, seq, lam_init):
    n, dw = q.shape
    nb = n // seq
    hd2 = dw // N_HEADS
    tq = min(ATTN_Q_TILE, seq)
    tk = min(ATTN_K_TILE, seq)
    nq = seq // tq
    has_ctx = k_ctx is not None
    in_specs = [_const_spec(lam_p.shape), _const_spec(sub_g.shape),
                pl.BlockSpec((tq, hd2), lambda b, h, i: (b * nq + i, h)),
                pl.BlockSpec((seq, hd2), lambda b, h, i: (b, h)),
                pl.BlockSpec((seq, hd2), lambda b, h, i: (b, h))]
    args = [lam_p, sub_g, q, k, v]
    scratch = [pltpu.VMEM((seq // tk, hd2, tk), BF16), pltpu.VMEM((2, hd2, tq), F32),
               pltpu.VMEM((ATTN_SCORE_BUFFERS, 2, tk, tq), F32)]
    if has_ctx:
        lc = k_ctx.shape[0] // nb
        in_specs += [pl.BlockSpec((lc, hd2), lambda b, h, i: (b, h))] * 2
        args += [k_ctx, v_ctx]
        scratch.append(pltpu.VMEM((hd2, lc), BF16))
    return pl.pallas_call(
        functools.partial(_attn_kernel, n_chunks=seq // tk, tk=tk, has_ctx=has_ctx, lam_init=lam_init),
        out_shape=jax.ShapeDtypeStruct((n, dw), BF16),
        grid=(nb, N_HEADS, nq),
        in_specs=in_specs,
        out_specs=pl.BlockSpec((tq, hd2), lambda b, h, i: (b * nq + i, h)),
        scratch_shapes=scratch,
        compiler_params=_cparams(("parallel", "parallel", "arbitrary")),
        name="attn_lat" if has_ctx else "attn_ctx",
    )(*args)


def _outproj_ffn_kernel(xc_ref, fc_ref, ac_ref, xl_ref, fl_ref, al_ref, wo_ref, mod_ref, g_ref, w1_ref, w3_ref,
                        w2_ref, o_ref, x_scr, *, ctx_tiles):
    m = mod_ref[0]

    def residual(x_ref, four_ref, attn_ref):
        fw = four_ref.shape[1]
        y = _dot(four_ref[...], wo_ref[0:fw, :]) + _dot(attn_ref[...], wo_ref[fw:, :])
        x_scr[...] = x_ref[...] + m[2:3] * y

    @pl.when(pl.program_id(0) < ctx_tiles)
    def _():
        residual(xc_ref, fc_ref, ac_ref)

    @pl.when(pl.program_id(0) >= ctx_tiles)
    def _():
        residual(xl_ref, fl_ref, al_ref)

    x = x_scr[...]
    h = _norm_mod(x, g_ref[...], m[3:4], m[4:5]).astype(BF16)
    a = (_silu(_dot(h, w1_ref[...])) * _dot(h, w3_ref[...])).astype(BF16)
    o_ref[...] = x + m[5:6] * _dot(a, w2_ref[...])


def _outproj_ffn(ctx, lat, w_out, mod, g, w1, w3, w2, *, seq):
    n_ctx, d = ctx[0].shape
    n_lat = lat[0].shape[0]
    tm = TOKEN_TILE
    tc = n_ctx // tm
    ctx_row = lambda i: (jnp.minimum(i, tc - 1), 0)
    lat_row = lambda i: (jnp.maximum(i - tc, 0), 0)
    in_specs = ([pl.BlockSpec((tm, a.shape[1]), ctx_row) for a in ctx]
                + [pl.BlockSpec((tm, a.shape[1]), lat_row) for a in lat]
                + [_const_spec(w_out.shape), pl.BlockSpec((1, 6, d), _seg_index_map(tm, n_ctx, seq)),
                   _const_spec((1, d)), _const_spec(w1.shape), _const_spec(w3.shape), _const_spec(w2.shape)])
    return pl.pallas_call(
        functools.partial(_outproj_ffn_kernel, ctx_tiles=tc),
        out_shape=jax.ShapeDtypeStruct((n_ctx + n_lat, d), F32),
        grid=((n_ctx + n_lat) // tm,), in_specs=in_specs,
        out_specs=pl.BlockSpec((tm, d), lambda i: (i, 0)),
        scratch_shapes=[pltpu.VMEM((tm, d), F32)],
        compiler_params=_cparams(("parallel",)),
        name="outproj_ffn",
    )(*ctx, *lat, w_out, mod, g, w1, w3, w2)


def _conv_kernel(x_ref, prev_ref, next_ref, g_ref, w1_ref, wdw_ref, bdw_ref, lng_ref, lnb_ref, w2_ref, mod_ref,
                 o_ref, win_scr, conv_scr, shift_scr, *, n_ctx, seq_ctx, seq_lat, taps, front):
    tm, d = x_ref.shape
    start = pl.program_id(0) * tm
    is_ctx = start < n_ctx
    off = jnp.where(is_ctx, start % seq_ctx, (start - n_ctx) % seq_lat)
    slen = jnp.where(is_ctx, seq_ctx, seq_lat)
    m = mod_ref[0]
    h = jnp.concatenate([_norm_mod(r[...], g_ref[...], m[0:1], m[1:2]).astype(BF16)
                         for r in (prev_ref, x_ref, next_ref)], axis=0)
    for cb in range(d // LANES):
        vg = _dot(h, w1_ref[:, 2 * cb * LANES:2 * (cb + 1) * LANES])
        u = vg[:, 0:LANES] * jax.nn.sigmoid(vg[:, LANES:2 * LANES])
        cols = slice(cb * LANES, (cb + 1) * LANES)
        win_scr[0:CONV_HALO, cols] = jnp.where(off == 0, 0.0, u[0:CONV_HALO])
        win_scr[CONV_HALO:CONV_HALO + tm, cols] = u[CONV_HALO:CONV_HALO + tm]
        win_scr[CONV_HALO + tm:, cols] = jnp.where(off + tm == slen, 0.0, u[CONV_HALO + tm:])
    n_q = wdw_ref.shape[0]
    rc = CONV_ROW_CHUNK
    for cb in range(d // LANES):
        cols = slice(cb * LANES, (cb + 1) * LANES)
        for k in range(1, SUBLANES):
            shift_scr[k] = win_scr[k:k + tm + n_q - SUBLANES, cols]
        w = {q: wdw_ref[q, :, cols] for q in range(front, front + taps)}
        bias = bdw_ref[:, cols]
        for r0 in range(0, tm, rc):
            acc = None
            for q in range(front, front + taps):
                k = q % SUBLANES
                if k:
                    tap = shift_scr[k, r0 + q - k:r0 + q - k + rc, :]
                else:
                    tap = win_scr[r0 + q:r0 + q + rc, cols]
                term = tap.reshape(rc // SUBLANES, SUBLANES, LANES) * w[q][None]
                acc = term if acc is None else acc + term
            conv_scr[r0:r0 + rc, cols] = acc.reshape(rc, LANES) + bias
    c = conv_scr[...]
    mu = jnp.mean(c, axis=-1, keepdims=True)
    cc = c - mu
    var = jnp.mean(cc * cc, axis=-1, keepdims=True)
    y = _silu(cc * lax.rsqrt(var + NORM_EPS) * lng_ref[...] + lnb_ref[...]).astype(BF16)
    o_ref[...] = x_ref[...] + m[2:3] * _dot(y, w2_ref[...])


def _conv(x_all, mod, g, w_pw1, w_dw, b_dw, ln_g, ln_b, w_pw2, *, n_ctx, seq_ctx, seq_lat):
    n, d = x_all.shape
    tm = min(CONV_TILE, seq_ctx)
    w_pw1 = w_pw1.reshape(d, 2, d // LANES, LANES).transpose(0, 2, 1, 3).reshape(d, 2 * d)
    taps = w_dw.shape[0]
    front = CONV_HALO - taps // 2
    w_dw = jnp.pad(w_dw, ((front, 2 * CONV_HALO - taps - front), (0, 0)))
    w_dw = jnp.broadcast_to(w_dw[:, None, :], (2 * CONV_HALO, SUBLANES, d))
    hb = tm // CONV_HALO
    last = n // CONV_HALO - 1
    row = lambda i: (i, 0)
    return pl.pallas_call(
        functools.partial(_conv_kernel, n_ctx=n_ctx, seq_ctx=seq_ctx, seq_lat=seq_lat, taps=taps, front=front),
        out_shape=jax.ShapeDtypeStruct((n, d), F32),
        grid=(n // tm,),
        in_specs=[pl.BlockSpec((tm, d), row),
                  pl.BlockSpec((CONV_HALO, d), lambda i: (jnp.maximum(i * hb - 1, 0), 0)),
                  pl.BlockSpec((CONV_HALO, d), lambda i: (jnp.minimum((i + 1) * hb, last), 0)),
                  _const_spec((1, d)), _const_spec(w_pw1.shape),
                  _const_spec(w_dw.shape), _const_spec((1, d)), _const_spec((1, d)), _const_spec((1, d)),
                  _const_spec(w_pw2.shape),
                  pl.BlockSpec((1, 6, d), _seg_index_map(tm, n_ctx, seq_lat))],
        out_specs=pl.BlockSpec((tm, d), row),
        scratch_shapes=[pltpu.VMEM((tm + 2 * CONV_HALO, d), F32), pltpu.VMEM((tm, d), F32),
                        pltpu.VMEM((SUBLANES, tm + 2 * CONV_HALO - SUBLANES, LANES), F32)],
        compiler_params=_cparams(("parallel",)),
        name="conv_module",
    )(x_all, x_all, x_all, g, w_pw1, w_dw, b_dw, ln_g, ln_b, w_pw2, mod)


def _router_kernel(x_ref, mod_ref, g_ref, wr_ref, tri_ref, upper_ref, h_ref, info_ref, runs_ref, *, n_experts):
    m = mod_ref[0]
    h = _norm_mod(x_ref[...], g_ref[...], m[3:4], m[4:5])
    h_hi = h.astype(BF16)
    h_ref[...] = h_hi
    h_lo = (h - h_hi.astype(F32)).astype(BF16)
    wr = wr_ref[...]
    w_hi = wr.astype(BF16)
    w_lo = (wr - w_hi.astype(F32)).astype(BF16)
    logits = _dot(h_hi, w_hi) + (_dot(h_lo, w_hi) + _dot(h_hi, w_lo))
    lane = lax.broadcasted_iota(jnp.int32, logits.shape, 1)
    neg = jnp.float32(-jnp.inf)
    logits = jnp.where(lane < n_experts, logits, neg)
    m1 = jnp.max(logits, axis=-1, keepdims=True)
    i1 = jnp.min(jnp.where(logits == m1, lane, LANES), axis=-1, keepdims=True)
    oh1 = lane == i1
    rest = jnp.where(oh1, neg, logits)
    m2 = jnp.max(rest, axis=-1, keepdims=True)
    i2 = jnp.min(jnp.where(rest == m2, lane, LANES), axis=-1, keepdims=True)
    oh2 = lane == i2
    e = jnp.exp(m2 - m1)
    g1 = 1.0 / (1.0 + e)
    g2 = e * g1
    oh = jnp.where(oh1, 1.0, 0.0) + jnp.where(oh2, 1.0, 0.0)
    cnt = jnp.sum(oh, axis=0, keepdims=True)
    run_units = jnp.floor((cnt + (RUN_ALIGN - 1.0)) * (1.0 / RUN_ALIGN))
    units8 = jnp.broadcast_to(run_units, (SUBLANES, LANES)).astype(BF16)
    run_start = _dot(units8, upper_ref[...])[0:1, :] * RUN_ALIGN
    before = _dot(tri_ref[...], oh.astype(BF16)) + run_start
    p1 = jnp.sum(jnp.where(oh1, before, 0.0), axis=-1, keepdims=True)
    p2 = jnp.sum(jnp.where(oh2, before, 0.0), axis=-1, keepdims=True)
    info = jnp.where(lane == 0, p1,
           jnp.where(lane == 1, p2,
           jnp.where(lane == 2, g1,
           jnp.where(lane == 3, g2, 0.0))))
    info_ref[...] = info
    sub = lax.broadcasted_iota(jnp.int32, (SUBLANES, LANES), 0)
    runs_ref[0] = jnp.where(sub == 0, run_units * RUN_ALIGN, jnp.where(sub == 1, run_start, 0.0))


def _router(x_all, mod, g, w_router, *, n_ctx, seq):
    n, d = x_all.shape
    n_experts = w_router.shape[1]
    tm = TOKEN_TILE
    wr = jnp.pad(w_router, ((0, 0), (0, LANES - n_experts)))
    tri = jnp.asarray(np.tril(np.ones((tm, tm), np.float32), -1), BF16)
    upper = jnp.asarray(np.triu(np.ones((LANES, LANES), np.float32), 1), BF16)
    row = lambda i: (i, 0)
    return pl.pallas_call(
        functools.partial(_router_kernel, n_experts=n_experts),
        out_shape=[jax.ShapeDtypeStruct((n, d), BF16), jax.ShapeDtypeStruct((n, LANES), F32),
                   jax.ShapeDtypeStruct((n // tm, SUBLANES, LANES), F32)],
        grid=(n // tm,),
        in_specs=[pl.BlockSpec((tm, d), row), pl.BlockSpec((1, 6, d), _seg_index_map(tm, n_ctx, seq)),
                  _const_spec((1, d)), _const_spec(wr.shape), _const_spec(tri.shape), _const_spec(upper.shape)],
        out_specs=[pl.BlockSpec((tm, d), row), pl.BlockSpec((tm, LANES), row),
                   pl.BlockSpec((1, SUBLANES, LANES), lambda i: (i, 0, 0))],
        compiler_params=_cparams(("parallel",)),
        name="moe_router",
    )(x_all, mod, g, wr, tri, upper)


def _run_copies(n_experts, len_ref, start_ref, dst_ref, tile, make_copy):
    ops = []
    for e in range(n_experts):
        n = len_ref[tile * n_experts + e]
        src = start_ref[tile * n_experts + e]
        dst = dst_ref[tile * n_experts + e]
        ops.append((n > 0, make_copy(pl.multiple_of(src, RUN_ALIGN), pl.multiple_of(dst, RUN_ALIGN),
                                     pl.multiple_of(n, RUN_ALIGN))))
    return ops


def _dispatch_kernel(zrow_ref, len_ref, start_ref, dst_ref, h_ref, info_ref, xg_ref, zero_scr, xs_scr, sem, zsem,
                     *, n_zero, n_experts):
    tm = h_ref.shape[0]
    tz = zero_scr.shape[0]
    rows = xs_scr.shape[1]

    def zero_copy(e):
        return pltpu.make_async_copy(zero_scr, xg_ref.at[pl.ds(pl.multiple_of(zrow_ref[e], tz), tz)], zsem)

    @pl.when(pl.program_id(0) == 0)
    def _():
        zero_scr[...] = jnp.zeros_like(zero_scr)
        for e in range(n_zero):
            @pl.when(zrow_ref[e] >= 0)
            def _():
                zero_copy(e).start()
        for e in range(n_zero):
            @pl.when(zrow_ref[e] >= 0)
            def _():
                zero_copy(e).wait()

    info_t = info_ref[...].T
    slot1 = info_t[0:1, :].astype(jnp.int32)
    slot2 = info_t[1:2, :].astype(jnp.int32)
    r = lax.broadcasted_iota(jnp.int32, (rows, tm), 0)
    perm = (jnp.where(r == slot1, 1.0, 0.0) + jnp.where(r == slot2, 1.0, 0.0)).astype(BF16)
    step = pl.program_id(0)
    buf = step % 2
    xs_scr[buf] = _dot(perm, h_ref[...])

    def run_writes(tile, b):
        def make_copy(src, dst, size):
            return pltpu.make_async_copy(xs_scr.at[b, pl.ds(src, size)], xg_ref.at[pl.ds(dst, size)], sem.at[b])
        return _run_copies(n_experts, len_ref, start_ref, dst_ref, tile, make_copy)

    for cond, copy in run_writes(step, buf):
        pl.when(cond)(copy.start)

    @pl.when(step > 0)
    def _():
        for cond, copy in run_writes(step - 1, 1 - buf):
            pl.when(cond)(copy.wait)

    @pl.when(step == pl.num_programs(0) - 1)
    def _():
        for cond, copy in run_writes(step, buf):
            pl.when(cond)(copy.wait)


def _dispatch(h_all, info, run_len, run_start, run_dst, zero_tile_row, n_rows):
    n, d = h_all.shape
    tm = TOKEN_TILE
    n_zero = zero_tile_row.shape[0]
    n_experts = run_len.shape[0] // (n // tm)
    grid_spec = pltpu.PrefetchScalarGridSpec(
        num_scalar_prefetch=4,
        grid=(n // tm,),
        in_specs=[pl.BlockSpec((tm, d), lambda i, *_: (i, 0)),
                  pl.BlockSpec((tm, LANES), lambda i, *_: (i, 0))],
        out_specs=pl.BlockSpec(memory_space=pl.ANY),
        scratch_shapes=[pltpu.VMEM((MOE_TILE, d), F32), pltpu.VMEM((2, _local_rows(tm, n_experts), d), F32),
                        pltpu.SemaphoreType.DMA((2,)), pltpu.SemaphoreType.DMA(())],
    )
    return pl.pallas_call(
        functools.partial(_dispatch_kernel, n_zero=n_zero, n_experts=n_experts),
        out_shape=jax.ShapeDtypeStruct((n_rows, d), F32),
        grid_spec=grid_spec,
        compiler_params=_cparams(("arbitrary",)),
        name="moe_dispatch",
    )(zero_tile_row, run_len, run_start, run_dst, h_all, info)


def _experts_kernel(te_ref, nt_ref, x_ref, w1_ref, w3_ref, w2_ref, o_ref, xb_scr):
    c = pl.program_id(1)

    @pl.when(pl.program_id(0) < nt_ref[0])
    def _():
        @pl.when(c == 0)
        def _():
            xb_scr[...] = x_ref[...].astype(BF16)
        xb = xb_scr[...]
        a = (_silu(_dot(xb, w1_ref[0])) * _dot(xb, w3_ref[0])).astype(BF16)
        y = _dot(a, w2_ref[0])

        @pl.when(c == 0)
        def _():
            o_ref[...] = y

        @pl.when(c > 0)
        def _():
            o_ref[...] += y

    @pl.when(jnp.logical_and(pl.program_id(0) >= nt_ref[0], c == 0))
    def _():
        o_ref[...] = jnp.zeros_like(o_ref)


def _experts(xg, w1, w3, w2, tile_expert, n_tiles_used):
    p, d = xg.shape
    tm = MOE_TILE
    ff = w1.shape[2]
    fc = MOE_FF_CHUNK if ff % MOE_FF_CHUNK == 0 else ff
    nc = ff // fc
    n_tiles = p // tm
    wmode = {}

    def tile(t, nt):
        return jnp.minimum(t, nt[0] - 1)

    def chunk(t, c, nt):
        return jnp.where(t < nt[0], c, nc - 1)

    grid_spec = pltpu.PrefetchScalarGridSpec(
        num_scalar_prefetch=2,
        grid=(n_tiles, nc),
        in_specs=[pl.BlockSpec((tm, d), lambda t, c, te, nt: (tile(t, nt), 0)),
                  pl.BlockSpec((1, d, fc), lambda t, c, te, nt: (te[tile(t, nt)], 0, chunk(t, c, nt)), **wmode),
                  pl.BlockSpec((1, d, fc), lambda t, c, te, nt: (te[tile(t, nt)], 0, chunk(t, c, nt)), **wmode),
                  pl.BlockSpec((1, fc, d), lambda t, c, te, nt: (te[tile(t, nt)], chunk(t, c, nt), 0), **wmode)],
        out_specs=pl.BlockSpec((tm, d), lambda t, c, te, nt: (t, 0)),
        scratch_shapes=[pltpu.VMEM((tm, d), BF16)],
    )
    return pl.pallas_call(
        _experts_kernel,
        out_shape=jax.ShapeDtypeStruct((p, d), F32),
        grid_spec=grid_spec,
        compiler_params=_cparams(("arbitrary", "arbitrary"), vmem=MOE_VMEM_LIMIT),
        name="moe_experts",
    )(tile_expert, n_tiles_used, xg, w1, w3, w2)


def _combine_kernel(len_ref, start_ref, dst_ref, yg_ref, x_ref, info_ref, mod_ref, fg_ref, o_ref, ys_scr, sem,
                    *, n_experts, tile0):
    tm = x_ref.shape[0]
    rows = ys_scr.shape[1]
    step = pl.program_id(0)
    buf = step % 2

    def run_reads(tile, b):
        def make_copy(src, dst, size):
            return pltpu.make_async_copy(yg_ref.at[pl.ds(dst, size)], ys_scr.at[b, pl.ds(src, size)], sem.at[b])
        return _run_copies(n_experts, len_ref, start_ref, dst_ref, tile0 + tile, make_copy)

    @pl.when(step == 0)
    def _():
        ys_scr[...] = jnp.zeros_like(ys_scr)
        for cond, copy in run_reads(step, buf):
            pl.when(cond)(copy.start)

    @pl.when(step + 1 < pl.num_programs(0))
    def _():
        for cond, copy in run_reads(step + 1, 1 - buf):
            pl.when(cond)(copy.start)

    for cond, copy in run_reads(step, buf):
        pl.when(cond)(copy.wait)
    info = info_ref[...]
    ys = ys_scr[buf].astype(BF16)
    lane = lax.broadcasted_iota(jnp.int32, (tm, rows), 1)
    pick1 = jnp.where(lane == info[:, 0:1].astype(jnp.int32), 1.0, 0.0).astype(BF16)
    pick2 = jnp.where(lane == info[:, 1:2].astype(jnp.int32), 1.0, 0.0).astype(BF16)
    y = info[:, 2:3] * _dot(pick1, ys) + info[:, 3:4] * _dot(pick2, ys)
    x = x_ref[...] + mod_ref[0][5:6] * y
    o_ref[...] = x * lax.rsqrt(jnp.mean(x * x, axis=-1, keepdims=True) + NORM_EPS) * fg_ref[...]


def _combine(yg, x_all, info, run_len, run_start, run_dst, mod, final_g, *, row0, n_rows, seq, mod_row0):
    n, d = x_all.shape
    tm = TOKEN_TILE
    t0 = row0 // tm
    per_seq = seq // tm
    n_experts = run_len.shape[0] // (n // tm)
    grid_spec = pltpu.PrefetchScalarGridSpec(
        num_scalar_prefetch=3,
        grid=(n_rows // tm,),
        in_specs=[pl.BlockSpec(memory_space=pl.ANY),
                  pl.BlockSpec((tm, d), lambda i, *_: (t0 + i, 0)),
                  pl.BlockSpec((tm, LANES), lambda i, *_: (t0 + i, 0)),
                  pl.BlockSpec((1, 6, d), lambda i, *_: (mod_row0 + (i // per_seq if mod_row0 else 0), 0, 0)),
                  pl.BlockSpec((1, d), lambda i, *_: (0, 0))],
        out_specs=pl.BlockSpec((tm, d), lambda i, *_: (i, 0)),
        scratch_shapes=[pltpu.VMEM((2, _local_rows(tm, n_experts), d), F32), pltpu.SemaphoreType.DMA((2,))],
    )
    return pl.pallas_call(
        functools.partial(_combine_kernel, n_experts=n_experts, tile0=t0),
        out_shape=jax.ShapeDtypeStruct((n_rows, d), F32),
        grid_spec=grid_spec,
        compiler_params=_cparams(("arbitrary",)),
        name="moe_combine",
    )(run_len, run_start, run_dst, yg, x_all, info, mod, final_g)


def _moe(x_all, mod, g, w_router, w1, w3, w2, final_g, *, n_ctx, seq_ctx_total, seq_lat):
    n, d = x_all.shape
    n_experts = w_router.shape[1]
    tm = TOKEN_TILE
    h_all, info, runs = _router(x_all, mod, g, w_router, n_ctx=n_ctx, seq=seq_lat)
    run_len = runs[:, 0, :n_experts].astype(jnp.int32)
    run_start = runs[:, 1, :n_experts].astype(jnp.int32)
    cnt = jnp.sum(run_len, axis=0)
    padded = ((cnt + MOE_TILE - 1) // MOE_TILE) * MOE_TILE
    ends = jnp.cumsum(padded)
    offs = ends - padded
    run_dst = offs[None, :] + jnp.cumsum(run_len, axis=0) - run_len
    n_tiles = -(-(TOP_K * n + (n // tm) * n_experts * (RUN_ALIGN - 1)) // MOE_TILE) + n_experts
    n_used = (ends[-1] // MOE_TILE).astype(jnp.int32)
    tile_start = jnp.arange(n_tiles, dtype=jnp.int32) * MOE_TILE
    tile_expert = jnp.minimum(jnp.sum(tile_start[:, None] >= ends[None, :], axis=1), n_experts - 1).astype(jnp.int32)
    tile_expert = jnp.where(tile_start < ends[-1], tile_expert, tile_expert[jnp.maximum(n_used - 1, 0)])
    tail_start = tile_start[(TOP_K * n) // MOE_TILE:]
    zero_tile_row = jnp.concatenate([jnp.where(padded > 0, ends - MOE_TILE, -1),
                                     jnp.where(tail_start >= ends[-1], tail_start, -1)]).astype(jnp.int32)
    run_args = (run_len.reshape(-1), run_start.reshape(-1), run_dst.reshape(-1).astype(jnp.int32))
    xg = _dispatch(h_all, info, *run_args, zero_tile_row, n_tiles * MOE_TILE)
    yg = _experts(xg, w1, w3, w2, tile_expert, n_used.reshape(1))
    y_ctx = _combine(yg, x_all, info, *run_args, mod, final_g, row0=0, n_rows=n_ctx, seq=seq_ctx_total, mod_row0=0)
    y_lat = _combine(yg, x_all, info, *run_args, mod, final_g, row0=n_ctx, n_rows=n - n_ctx, seq=seq_lat,
                     mod_row0=1)
    return y_ctx, y_lat


def _rope_tables(seq, hd):
    axis_dim = hd // 2
    nf = axis_dim // 2
    inv = jnp.power(ROPE_BASE, -jnp.arange(nf, dtype=F32) / nf)
    pos = np.arange(seq)
    row = jnp.asarray(pos // GRID_W, F32)
    col = jnp.asarray(pos % GRID_W, F32)
    lane = np.arange(2 * hd)
    dd = lane % hd
    use_col = (dd // axis_dim) == 1
    e = dd % axis_dim
    first = jnp.asarray(e < nf)[None, :]
    freq = inv[e % nf]
    ang = jnp.where(jnp.asarray(use_col)[None, :], col[:, None], row[:, None]) * freq[None, :]
    cos, sin = jnp.cos(ang), jnp.sin(ang)
    return cos, jnp.where(first, -sin, 0.0), jnp.where(first, 0.0, sin)


def _channel_dft(fw):
    g = FOURIER_GROUP_DIM
    ang = (2.0 * np.pi / g) * ((np.arange(g)[:, None] * np.arange(g)[None, :]) % g)
    eye = np.eye(fw // g)
    return jnp.asarray(np.concatenate([np.kron(eye, np.cos(ang)), np.kron(eye, np.sin(ang))], axis=1), BF16)


def kernel(x_prompt, x_sample, cache_k, cache_v, c, c_ctx, w_ada, b_ada, norm_g, final_g, w_in_ab, w_out_ab,
           diff_lambda, diff_subln, conv_w_pw1, conv_w_dw, conv_b_dw, conv_ln_g, conv_ln_b, conv_w_pw2,
           ffn_w1, ffn_w3, ffn_w2, moe_router, moe_w1, moe_w3, moe_w2):
    bp, lp, d = x_prompt.shape
    bs, ls, _ = x_sample.shape
    lc = cache_k.shape[2]
    n_ctx, n_lat = bp * lp, bs * ls
    n_all = n_ctx + n_lat
    dw = N_HEADS * cache_k.shape[-1]
    hd = cache_k.shape[-1] // 2
    fw = w_in_ab.shape[2] - 3 * dw

    rows = -(-(1 + bs) // SUBLANES) * SUBLANES
    cond = jnp.zeros((rows, d), F32).at[0].set(c_ctx).at[1:1 + bs].set(c)
    mod = _modulation(cond, w_ada, b_ada)

    lam_init = 0.8 - 0.6 * math.exp(-0.3 * 0)
    q_scale = (hd ** -0.5) * LOG2E
    w_in = w_in_ab[0].astype(BF16)
    bd = _channel_dft(fw)
    g00 = norm_g[0, 0].reshape(1, d)
    xp = x_prompt.reshape(n_ctx, d)
    xs = x_sample.reshape(n_lat, d)
    acas_p, q_p, k_p, v_p, kc_new, vc_new = _inproj(
        xp, mod[0], g00, w_in, bd, seq=lp, mod_row0=0, rope_tables=None, cache=True, q_scale=q_scale)
    acas_s, q_s, k_s, v_s = _inproj(
        xs, mod[0], g00, w_in, bd, seq=ls, mod_row0=1, rope_tables=_rope_tables(ls, hd), cache=False, q_scale=q_scale)
    four_p = _fourier_short(acas_p, lp)
    four_s = _fourier_long(acas_s, ls)
    lam_p = diff_lambda[0]
    sub_g = diff_subln[0].reshape(1, 2 * hd)
    attn_p = _attention(lam_p, sub_g, q_p, k_p, v_p, None, None, seq=lp, lam_init=lam_init)
    k_ctx = cache_k[:, 0].reshape(bs * lc, dw).astype(BF16)
    v_ctx = cache_v[:, 0].reshape(bs * lc, dw).astype(BF16)
    attn_s = _attention(lam_p, sub_g, q_s, k_s, v_s, k_ctx, v_ctx, seq=ls, lam_init=lam_init)
    w_out = w_out_ab[0].astype(BF16)
    x_all = _outproj_ffn((xp, four_p, attn_p), (xs, four_s, attn_s), w_out, mod[0], norm_g[0, 1].reshape(1, d),
                         ffn_w1[0].astype(BF16), ffn_w3[0].astype(BF16), ffn_w2[0].astype(BF16), seq=ls)

    x_all = _conv(x_all, mod[1], norm_g[1, 0].reshape(1, d), conv_w_pw1[0].astype(BF16), conv_w_dw[0],
                  conv_b_dw[0].reshape(1, d), conv_ln_g[0].reshape(1, d), conv_ln_b[0].reshape(1, d),
                  conv_w_pw2[0].astype(BF16), n_ctx=n_ctx, seq_ctx=lp, seq_lat=ls)
    y_ctx, y_lat = _moe(x_all, mod[1], norm_g[1, 1].reshape(1, d), moe_router[0], moe_w1[0].astype(BF16),
                        moe_w3[0].astype(BF16), moe_w2[0].astype(BF16), final_g.reshape(1, d),
                        n_ctx=n_ctx, seq_ctx_total=n_ctx, seq_lat=ls)

    n_even = w_in_ab.shape[0]
    new_k = kc_new.reshape(bp, n_even, lp, N_HEADS, 2 * hd)
    new_v = vc_new.reshape(bp, n_even, lp, N_HEADS, 2 * hd)
    return (y_ctx.reshape(bp, lp, d), y_lat.reshape(bs, ls, d), new_k, new_v)
```

```python
import functools
import math

import numpy as np
import jax
import jax.numpy as jnp
from jax import lax
from jax.experimental import pallas as pl
from jax.experimental.pallas import tpu as pltpu

F32 = jnp.float32
BF16 = jnp.bfloat16

GRID_W = 64
ROPE_BASE = 10000.0
N_HEADS = 4
FOURIER_GROUP_DIM = 128
CONV_HALO = 16
TOP_K = 2
NORM_EPS = 1e-6
SUBLN_EPS = 1e-5
LANES = 128
SUBLANES = 8
LOG2E = 1.4426950408889634

TOKEN_TILE = 512
CONV_TILE = 256
CONV_ROW_CHUNK = 64
ATTN_Q_TILE = 512
ATTN_K_TILE = 512
ATTN_SCORE_BUFFERS = 2
FOURIER_ROW_TILE = 512
FOURIER_TABLE_ROWS = 256
MOE_TILE = 512
MOE_FF_CHUNK = 3584
RUN_ALIGN = SUBLANES
VMEM_LIMIT = 56 * 1024 * 1024
MOE_VMEM_LIMIT = 62 * 1024 * 1024


def _local_rows(tile, n_experts):
    return -(-(TOP_K * tile + n_experts * (RUN_ALIGN - 1)) // LANES) * LANES


def _cparams(semantics, vmem=VMEM_LIMIT):
    return pltpu.CompilerParams(dimension_semantics=semantics, vmem_limit_bytes=vmem)


def _const_spec(shape):
    nd = len(shape)
    return pl.BlockSpec(shape, lambda *_: (0,) * nd, pipeline_mode=pl.Buffered(1))


def _dot(a, b):
    return jnp.dot(a, b, preferred_element_type=F32)


def _silu(x):
    return x * jax.nn.sigmoid(x)


def _norm_mod(x, g, shift, scale):
    ms = jnp.mean(x * x, axis=-1, keepdims=True)
    return (x * lax.rsqrt(ms + NORM_EPS)) * (g * (1.0 + scale)) + shift


def _seg_index_map(tile, n_ctx, seq):
    def index_map(i, *_):
        start = i * tile
        return (jnp.where(start < n_ctx, 0, 1 + (start - n_ctx) // seq), 0, 0)
    return index_map


def _mod_kernel(c_ref, w_ref, b_ref, o_ref):
    c = c_ref[...]
    o_ref[0] = _dot(_silu(c).astype(BF16), w_ref[0].astype(BF16)) + b_ref[0]


def _modulation(cond, w_ada, b_ada):
    depth, d, n6 = w_ada.shape
    rows = cond.shape[0]
    tn = 1536
    out = pl.pallas_call(
        _mod_kernel,
        out_shape=jax.ShapeDtypeStruct((depth, rows, n6), F32),
        grid=(depth, n6 // tn),
        in_specs=[pl.BlockSpec((rows, d), lambda l, j: (0, 0)),
                  pl.BlockSpec((1, d, tn), lambda l, j: (l, 0, j)),
                  pl.BlockSpec((1, 1, tn), lambda l, j: (l, 0, j))],
        out_specs=pl.BlockSpec((1, rows, tn), lambda l, j: (l, 0, j)),
        compiler_params=_cparams(("parallel", "parallel")),
        name="modulation",
    )(cond, w_ada, b_ada.reshape(depth, 1, n6))
    return out.reshape(depth, rows, 6, d)


def _rope(t, cos, sin_lo, sin_hi):
    outs = []
    for h in range(t.shape[1] // LANES):
        th = t[:, h * LANES:(h + 1) * LANES]
        outs.append(th * cos + pltpu.roll(th, LANES - 16, 1) * sin_lo + pltpu.roll(th, 16, 1) * sin_hi)
    return jnp.concatenate(outs, axis=-1)


def _inproj_kernel(*refs, rope, cache, q_scale):
    x_ref, mod_ref, g_ref, w_ref, bd_ref = refs[:5]
    pos = 5
    if rope:
        cos_ref, slo_ref, shi_ref = refs[pos:pos + 3]
        pos += 3
    acas_ref, q_ref, k_ref, v_ref = refs[pos:pos + 4]
    m = mod_ref[0]
    h = _norm_mod(x_ref[...], g_ref[...], m[0:1], m[1:2]).astype(BF16)
    fw = bd_ref.shape[0]
    dw = q_ref.shape[1]
    f = _dot(h, w_ref[:, 0:fw])
    acas_ref[...] = _dot(f.astype(BF16), bd_ref[...]).astype(BF16)
    q = _dot(h, w_ref[:, fw:fw + dw])
    k = _dot(h, w_ref[:, fw + dw:fw + 2 * dw])
    v = _dot(h, w_ref[:, fw + 2 * dw:fw + 3 * dw])
    if cache:
        kc_ref, vc_ref = refs[pos + 4:pos + 6]
        kc_ref[...] = k
        vc_ref[...] = v
    if rope:
        cos, slo, shi = cos_ref[...], slo_ref[...], shi_ref[...]
        q = _rope(q, cos, slo, shi)
        k = _rope(k, cos, slo, shi)
    q_ref[...] = (q * q_scale).astype(BF16)
    k_ref[...] = k.astype(BF16)
    v_ref[...] = v.astype(BF16)


def _inproj(x, mod, g, w_in, bd, *, seq, mod_row0, rope_tables, cache, q_scale):
    n, d = x.shape
    tm = min(TOKEN_TILE, seq)
    fw = bd.shape[0]
    dw = (w_in.shape[1] - fw) // 3
    per_seq = seq // tm
    row = lambda i: (i, 0)
    in_specs = [pl.BlockSpec((tm, d), row),
                pl.BlockSpec((1, 6, d), lambda i: (mod_row0 + (i // per_seq if mod_row0 else 0), 0, 0)),
                _const_spec((1, d)), _const_spec(w_in.shape), _const_spec(bd.shape)]
    args = [x, mod, g, w_in, bd]
    if rope_tables is not None:
        in_specs += [pl.BlockSpec((tm, LANES), lambda i: (i % per_seq, 0))] * 3
        args += list(rope_tables)
    out_shape = [jax.ShapeDtypeStruct((n, 2 * fw), BF16)] + [jax.ShapeDtypeStruct((n, dw), BF16)] * 3
    out_specs = [pl.BlockSpec((tm, 2 * fw), row)] + [pl.BlockSpec((tm, dw), row)] * 3
    if cache:
        out_shape += [jax.ShapeDtypeStruct((n, dw), F32)] * 2
        out_specs += [pl.BlockSpec((tm, dw), row)] * 2
    return pl.pallas_call(
        functools.partial(_inproj_kernel, rope=rope_tables is not None, cache=cache, q_scale=q_scale),
        out_shape=out_shape, grid=(n // tm,), in_specs=in_specs, out_specs=out_specs,
        compiler_params=_cparams(("parallel",)),
        name="inproj_ctx" if cache else "inproj_lat",
    )(*args)


def _fourier_short_kernel(ct_ref, st_ref, acas_ref, o_ref, *, scale):
    fw = o_ref.shape[1]
    y = _dot(ct_ref[...], acas_ref[:, 0:fw]) - _dot(st_ref[...], acas_ref[:, fw:2 * fw])
    o_ref[...] = (y * scale).astype(BF16)


def _fourier_short(acas, seq):
    n, fw2 = acas.shape
    fw = fw2 // 2
    idx = (np.arange(seq)[:, None] * np.arange(seq)[None, :]) % seq
    ang = (2.0 * np.pi / seq) * idx
    ct = jnp.asarray(np.cos(ang), F32).astype(BF16)
    st = jnp.asarray(np.sin(ang), F32).astype(BF16)
    scale = 1.0 / math.sqrt(seq * FOURIER_GROUP_DIM)
    return pl.pallas_call(
        functools.partial(_fourier_short_kernel, scale=scale),
        out_shape=jax.ShapeDtypeStruct((n, fw), BF16),
        grid=(n // seq,),
        in_specs=[_const_spec((seq, seq)), _const_spec((seq, seq)),
                  pl.BlockSpec((seq, fw2), lambda b: (b, 0))],
        out_specs=pl.BlockSpec((seq, fw), lambda b: (b, 0)),
        compiler_params=_cparams(("parallel",)),
        name="fourier_ctx",
    )(ct, st, acas)


def _fourier_fold_kernel(flip_ref, acas_ref, o_ref, mid_ref):
    seq, fw2 = acas_ref.shape
    fw = fw2 // 2
    tb = flip_ref.shape[0]
    sign = jnp.where(lax.broadcasted_iota(jnp.int32, (1, fw2), 1) < fw, 1.0, -1.0)
    row = lax.broadcasted_iota(jnp.int32, (tb, fw2), 0)
    for l0 in range(0, seq // 2, tb):
        mirrored = _dot(flip_ref[...], acas_ref[seq - l0 - tb:seq - l0, :])
        if l0:
            mirrored = jnp.where(row == 0, acas_ref[seq - l0:seq - l0 + 1, :].astype(F32), mirrored)
        o_ref[l0:l0 + tb, :] = (acas_ref[l0:l0 + tb, :].astype(F32) + sign * mirrored).astype(BF16)
    mid_ref[0] = jnp.broadcast_to(acas_ref[seq // 2:seq // 2 + 1, 0:fw].astype(F32), (SUBLANES, fw))


def _fourier_long_kernel(cb_ref, sb_ref, ca_ref, sa_ref, fold_ref, mid_ref, o_ref, ct_scr, st_scr, *, scale):
    @pl.when(pl.program_id(1) == 0)
    def _():
        tb, half = cb_ref.shape
        cw = 512
        for part in range(ca_ref.shape[0]):
            rows = slice(part * tb, (part + 1) * tb)
            for c0 in range(0, half, cw):
                ca, sa = ca_ref[part, :, c0:c0 + cw], sa_ref[part, :, c0:c0 + cw]
                cb, sb = cb_ref[:, c0:c0 + cw], sb_ref[:, c0:c0 + cw]
                ct_scr[rows, c0:c0 + cw] = (ca * cb - sa * sb).astype(BF16)
                st_scr[rows, c0:c0 + cw] = (sa * cb + ca * sb).astype(BF16)
    tr, fw = o_ref.shape
    y = _dot(ct_scr[...], fold_ref[:, 0:fw]) - _dot(st_scr[...], fold_ref[:, fw:2 * fw])
    odd = (lax.broadcasted_iota(jnp.int32, (tr, fw), 0) & 1) == 1
    y = y + jnp.where(odd, -mid_ref[0][0:1, :], mid_ref[0][0:1, :])
    o_ref[...] = (y * scale).astype(BF16)


def _fourier_long(acas, seq):
    n, fw2 = acas.shape
    fw = fw2 // 2
    nb = n // seq
    half = seq // 2
    tr = min(FOURIER_ROW_TILE, seq)
    tb = min(FOURIER_TABLE_ROWS, tr)
    parts = tr // tb
    nr = seq // tr
    flip = np.zeros((tb, tb), np.float32)
    flip[np.arange(1, tb), tb - np.arange(1, tb)] = 1.0
    folded, mid = pl.pallas_call(
        _fourier_fold_kernel,
        out_shape=[jax.ShapeDtypeStruct((nb * half, fw2), BF16), jax.ShapeDtypeStruct((nb, SUBLANES, fw), F32)],
        grid=(nb,),
        in_specs=[_const_spec((tb, tb)), pl.BlockSpec((seq, fw2), lambda b: (b, 0))],
        out_specs=[pl.BlockSpec((half, fw2), lambda b: (b, 0)), pl.BlockSpec((1, SUBLANES, fw), lambda b: (b, 0, 0))],
        compiler_params=_cparams(("parallel",)),
        name="fourier_fold",
    )(jnp.asarray(flip, BF16), acas)
    col = np.arange(half)[None, :]
    beta = (2.0 * np.pi / seq) * ((np.arange(tb)[:, None] * col) % seq)
    alpha = (2.0 * np.pi / seq) * (((np.arange(seq // tb) * tb)[:, None] * col) % seq)
    cb, sb = jnp.asarray(np.cos(beta), F32), jnp.asarray(np.sin(beta), F32)
    ca = jnp.asarray(np.cos(alpha), F32).reshape(seq // tb, 1, half)
    sa = jnp.asarray(np.sin(alpha), F32).reshape(seq // tb, 1, half)
    scale = 1.0 / math.sqrt(seq * FOURIER_GROUP_DIM)
    return pl.pallas_call(
        functools.partial(_fourier_long_kernel, scale=scale),
        out_shape=jax.ShapeDtypeStruct((n, fw), BF16),
        grid=(nr, nb),
        in_specs=[_const_spec((tb, half)), _const_spec((tb, half)),
                  pl.BlockSpec((parts, 1, half), lambda r, b: (r, 0, 0)),
                  pl.BlockSpec((parts, 1, half), lambda r, b: (r, 0, 0)),
                  pl.BlockSpec((half, fw2), lambda r, b: (b, 0)),
                  pl.BlockSpec((1, SUBLANES, fw), lambda r, b: (b, 0, 0))],
        out_specs=pl.BlockSpec((tr, fw), lambda r, b: (b * nr + r, 0)),
        scratch_shapes=[pltpu.VMEM((tr, half), BF16), pltpu.VMEM((tr, half), BF16)],
        compiler_params=_cparams(("arbitrary", "arbitrary")),
        name="fourier_lat",
    )(cb, sb, ca, sa, folded, mid)


def _col_reduce(x, op):
    r = x.reshape(x.shape[0] // SUBLANES, SUBLANES, x.shape[1])
    n = r.shape[0]
    while n > 1:
        n //= 2
        r = op(r[:n], r[n:2 * n])
    r = r[0]
    for shift in (4, 2, 1):
        r = op(r, pltpu.roll(r, shift, 0))
    return r[0:1]


def _attn_kernel(*refs, n_chunks, tk, has_ctx, lam_init):
    lam_ref, sub_ref, q_ref, k_ref, v_ref = refs[:5]
    pos = 5
    if has_ctx:
        kc_ref, vc_ref = refs[5:7]
        pos = 7
    o_ref, vt_scr, acc_scr, s_scr = refs[pos:pos + 4]
    if has_ctx:
        vtc_scr = refs[pos + 4]
    tq, hd2 = q_ref.shape
    half = hd2 // 2
    depth = s_scr.shape[0]

    @pl.when(pl.program_id(2) == 0)
    def _():
        for c in range(n_chunks):
            vt_scr[c] = v_ref[c * tk:(c + 1) * tk, :].astype(F32).T.astype(BF16)
        if has_ctx:
            vtc_scr[...] = vc_ref[...].astype(F32).T.astype(BF16)

    qt = q_ref[...].astype(F32).T
    row = lax.broadcasted_iota(jnp.int32, qt.shape, 0)
    qts = (jnp.where(row < half, qt, 0.0).astype(BF16),
           jnp.where(row >= half, qt, 0.0).astype(BF16))

    chunks = [(k_ref, c * tk, tk, functools.partial(vt_scr.__getitem__, c)) for c in range(n_chunks)]
    if has_ctx:
        chunks.append((kc_ref, 0, kc_ref.shape[0], functools.partial(vtc_scr.__getitem__, Ellipsis)))

    chunk_max = {}

    def scores(idx):
        k_src, start, size, _ = chunks[idx]
        kc = k_src[start:start + size, :]
        for j in range(2):
            s = _dot(kc, qts[j])
            s_scr[idx % depth, j, 0:size, :] = s
            chunk_max[idx, j] = _col_reduce(s, jnp.maximum)

    def softmax_pv(idx, state):
        _, _, size, vt = chunks[idx]
        new = []
        for j in range(2):
            s = s_scr[idx % depth, j, 0:size, :]
            m_blk = chunk_max[idx, j]
            if state is None:
                m_new = m_blk
                p = jnp.exp2(s - m_new)
                l_new = _col_reduce(p, jnp.add)
                acc_scr[j] = _dot(vt(), p.astype(BF16))
            else:
                m, l = state[j]
                m_new = jnp.maximum(m, m_blk)
                alpha = jnp.exp2(m - m_new)
                p = jnp.exp2(s - m_new)
                l_new = alpha * l + _col_reduce(p, jnp.add)
                acc_scr[j] = alpha * acc_scr[j] + _dot(vt(), p.astype(BF16))
            new.append((m_new, l_new))
        return tuple(new)

    state = None
    ahead = depth - 1
    for idx in range(min(ahead, len(chunks))):
        scores(idx)
    for idx in range(len(chunks)):
        if idx + ahead < len(chunks):
            scores(idx + ahead)
        state = softmax_pv(idx, state)
    (_, l1), (_, l2) = state
    lp = lam_ref[...]
    lam = (jnp.exp(jnp.sum(lp[0:1] * lp[1:2], axis=-1, keepdims=True))
           - jnp.exp(jnp.sum(lp[2:3] * lp[3:4], axis=-1, keepdims=True)) + lam_init)
    ot = acc_scr[0] * (1.0 / l1) - lam * (acc_scr[1] * (1.0 / l2))
    o = ot.T
    o = o * lax.rsqrt(jnp.mean(o * o, axis=-1, keepdims=True) + SUBLN_EPS)
    o_ref[...] = (o * sub_ref[...] * (1.0 - lam_init)).astype(BF16)


def _attention(lam_p, sub_g, q, k, v, k_ctx, v_ctx, *, seq, lam_init):
    n, dw = q.shape
    nb = n // seq
    hd2 = dw // N_HEADS
    tq = min(ATTN_Q_TILE, seq)
    tk = min(ATTN_K_TILE, seq)
    nq = seq // tq
    has_ctx = k_ctx is not None
    in_specs = [_const_spec(lam_p.shape), _const_spec(sub_g.shape),
                pl.BlockSpec((tq, hd2), lambda b, h, i: (b * nq + i, h)),
                pl.BlockSpec((seq, hd2), lambda b, h, i: (b, h)),
                pl.BlockSpec((seq, hd2), lambda b, h, i: (b, h))]
    args = [lam_p, sub_g, q, k, v]
    scratch = [pltpu.VMEM((seq // tk, hd2, tk), BF16), pltpu.VMEM((2, hd2, tq), F32),
               pltpu.VMEM((ATTN_SCORE_BUFFERS, 2, tk, tq), F32)]
    if has_ctx:
        lc = k_ctx.shape[0] // nb
        in_specs += [pl.BlockSpec((lc, hd2), lambda b, h, i: (b, h))] * 2
        args += [k_ctx, v_ctx]
        scratch.append(pltpu.VMEM((hd2, lc), BF16))
    return pl.pallas_call(
        functools.partial(_attn_kernel, n_chunks=seq // tk, tk=tk, has_ctx=has_ctx, lam_init=lam_init),
        out_shape=jax.ShapeDtypeStruct((n, dw), BF16),
        grid=(nb, N_HEADS, nq),
        in_specs=in_specs,
        out_specs=pl.BlockSpec((tq, hd2), lambda b, h, i: (b * nq + i, h)),
        scratch_shapes=scratch,
        compiler_params=_cparams(("parallel", "parallel", "arbitrary")),
        name="attn_lat" if has_ctx else "attn_ctx",
    )(*args)


def _outproj_ffn_kernel(xc_ref, fc_ref, ac_ref, xl_ref, fl_ref, al_ref, wo_ref, mod_ref, g_ref, w1_ref, w3_ref,
                        w2_ref, o_ref, x_scr, *, ctx_tiles):
    m = mod_ref[0]

    def residual(x_ref, four_ref, attn_ref):
        fw = four_ref.shape[1]
        y = _dot(four_ref[...], wo_ref[0:fw, :]) + _dot(attn_ref[...], wo_ref[fw:, :])
        x_scr[...] = x_ref[...] + m[2:3] * y

    @pl.when(pl.program_id(0) < ctx_tiles)
    def _():
        residual(xc_ref, fc_ref, ac_ref)

    @pl.when(pl.program_id(0) >= ctx_tiles)
    def _():
        residual(xl_ref, fl_ref, al_ref)

    x = x_scr[...]
    h = _norm_mod(x, g_ref[...], m[3:4], m[4:5]).astype(BF16)
    a = (_silu(_dot(h, w1_ref[...])) * _dot(h, w3_ref[...])).astype(BF16)
    o_ref[...] = x + m[5:6] * _dot(a, w2_ref[...])


def _outproj_ffn(ctx, lat, w_out, mod, g, w1, w3, w2, *, seq):
    n_ctx, d = ctx[0].shape
    n_lat = lat[0].shape[0]
    tm = TOKEN_TILE
    tc = n_ctx // tm
    ctx_row = lambda i: (jnp.minimum(i, tc - 1), 0)
    lat_row = lambda i: (jnp.maximum(i - tc, 0), 0)
    in_specs = ([pl.BlockSpec((tm, a.shape[1]), ctx_row) for a in ctx]
                + [pl.BlockSpec((tm, a.shape[1]), lat_row) for a in lat]
                + [_const_spec(w_out.shape), pl.BlockSpec((1, 6, d), _seg_index_map(tm, n_ctx, seq)),
                   _const_spec((1, d)), _const_spec(w1.shape), _const_spec(w3.shape), _const_spec(w2.shape)])
    return pl.pallas_call(
        functools.partial(_outproj_ffn_kernel, ctx_tiles=tc),
        out_shape=jax.ShapeDtypeStruct((n_ctx + n_lat, d), F32),
        grid=((n_ctx + n_lat) // tm,), in_specs=in_specs,
        out_specs=pl.BlockSpec((tm, d), lambda i: (i, 0)),
        scratch_shapes=[pltpu.VMEM((tm, d), F32)],
        compiler_params=_cparams(("parallel",)),
        name="outproj_ffn",
    )(*ctx, *lat, w_out, mod, g, w1, w3, w2)


def _conv_kernel(x_ref, prev_ref, next_ref, g_ref, w1_ref, wdw_ref, bdw_ref, lng_ref, lnb_ref, w2_ref, mod_ref,
                 o_ref, win_scr, conv_scr, shift_scr, *, n_ctx, seq_ctx, seq_lat, taps, front):
    tm, d = x_ref.shape
    start = pl.program_id(0) * tm
    is_ctx = start < n_ctx
    off = jnp.where(is_ctx, start % seq_ctx, (start - n_ctx) % seq_lat)
    slen = jnp.where(is_ctx, seq_ctx, seq_lat)
    m = mod_ref[0]
    h = jnp.concatenate([_norm_mod(r[...], g_ref[...], m[0:1], m[1:2]).astype(BF16)
                         for r in (prev_ref, x_ref, next_ref)], axis=0)
    for cb in range(d // LANES):
        vg = _dot(h, w1_ref[:, 2 * cb * LANES:2 * (cb + 1) * LANES])
        u = vg[:, 0:LANES] * jax.nn.sigmoid(vg[:, LANES:2 * LANES])
        cols = slice(cb * LANES, (cb + 1) * LANES)
        win_scr[0:CONV_HALO, cols] = jnp.where(off == 0, 0.0, u[0:CONV_HALO])
        win_scr[CONV_HALO:CONV_HALO + tm, cols] = u[CONV_HALO:CONV_HALO + tm]
        win_scr[CONV_HALO + tm:, cols] = jnp.where(off + tm == slen, 0.0, u[CONV_HALO + tm:])
    n_q = wdw_ref.shape[0]
    rc = CONV_ROW_CHUNK
    for cb in range(d // LANES):
        cols = slice(cb * LANES, (cb + 1) * LANES)
        for k in range(1, SUBLANES):
            shift_scr[k] = win_scr[k:k + tm + n_q - SUBLANES, cols]
        w = {q: wdw_ref[q, :, cols] for q in range(front, front + taps)}
        bias = bdw_ref[:, cols]
        for r0 in range(0, tm, rc):
            acc = None
            for q in range(front, front + taps):
                k = q % SUBLANES
                if k:
                    tap = shift_scr[k, r0 + q - k:r0 + q - k + rc, :]
                else:
                    tap = win_scr[r0 + q:r0 + q + rc, cols]
                term = tap.reshape(rc // SUBLANES, SUBLANES, LANES) * w[q][None]
                acc = term if acc is None else acc + term
            conv_scr[r0:r0 + rc, cols] = acc.reshape(rc, LANES) + bias
    c = conv_scr[...]
    mu = jnp.mean(c, axis=-1, keepdims=True)
    cc = c - mu
    var = jnp.mean(cc * cc, axis=-1, keepdims=True)
    y = _silu(cc * lax.rsqrt(var + NORM_EPS) * lng_ref[...] + lnb_ref[...]).astype(BF16)
    o_ref[...] = x_ref[...] + m[2:3] * _dot(y, w2_ref[...])


def _conv(x_all, mod, g, w_pw1, w_dw, b_dw, ln_g, ln_b, w_pw2, *, n_ctx, seq_ctx, seq_lat):
    n, d = x_all.shape
    tm = min(CONV_TILE, seq_ctx)
    w_pw1 = w_pw1.reshape(d, 2, d // LANES, LANES).transpose(0, 2, 1, 3).reshape(d, 2 * d)
    taps = w_dw.shape[0]
    front = CONV_HALO - taps // 2
    w_dw = jnp.pad(w_dw, ((front, 2 * CONV_HALO - taps - front), (0, 0)))
    w_dw = jnp.broadcast_to(w_dw[:, None, :], (2 * CONV_HALO, SUBLANES, d))
    hb = tm // CONV_HALO
    last = n // CONV_HALO - 1
    row = lambda i: (i, 0)
    return pl.pallas_call(
        functools.partial(_conv_kernel, n_ctx=n_ctx, seq_ctx=seq_ctx, seq_lat=seq_lat, taps=taps, front=front),
        out_shape=jax.ShapeDtypeStruct((n, d), F32),
        grid=(n // tm,),
        in_specs=[pl.BlockSpec((tm, d), row),
                  pl.BlockSpec((CONV_HALO, d), lambda i: (jnp.maximum(i * hb - 1, 0), 0)),
                  pl.BlockSpec((CONV_HALO, d), lambda i: (jnp.minimum((i + 1) * hb, last), 0)),
                  _const_spec((1, d)), _const_spec(w_pw1.shape),
                  _const_spec(w_dw.shape), _const_spec((1, d)), _const_spec((1, d)), _const_spec((1, d)),
                  _const_spec(w_pw2.shape),
                  pl.BlockSpec((1, 6, d), _seg_index_map(tm, n_ctx, seq_lat))],
        out_specs=pl.BlockSpec((tm, d), row),
        scratch_shapes=[pltpu.VMEM((tm + 2 * CONV_HALO, d), F32), pltpu.VMEM((tm, d), F32),
                        pltpu.VMEM((SUBLANES, tm + 2 * CONV_HALO - SUBLANES, LANES), F32)],
        compiler_params=_cparams(("parallel",)),
        name="conv_module",
    )(x_all, x_all, x_all, g, w_pw1, w_dw, b_dw, ln_g, ln_b, w_pw2, mod)


def _router_kernel(x_ref, mod_ref, g_ref, wr_ref, tri_ref, upper_ref, h_ref, info_ref, runs_ref, *, n_experts):
    m = mod_ref[0]
    h = _norm_mod(x_ref[...], g_ref[...], m[3:4], m[4:5])
    h_hi = h.astype(BF16)
    h_ref[...] = h_hi
    h_lo = (h - h_hi.astype(F32)).astype(BF16)
    hi_both = _dot(h_hi, wr_ref[...])
    logits = hi_both[:, 0:LANES] + (_dot(h_lo, wr_ref[:, 0:LANES]) + hi_both[:, LANES:2 * LANES])
    lane = lax.broadcasted_iota(jnp.int32, logits.shape, 1)
    neg = jnp.float32(-jnp.inf)
    logits = jnp.where(lane < n_experts, logits, neg)
    m1 = jnp.max(logits, axis=-1, keepdims=True)
    i1 = jnp.min(jnp.where(logits == m1, lane, LANES), axis=-1, keepdims=True)
    oh1 = lane == i1
    rest = jnp.where(oh1, neg, logits)
    m2 = jnp.max(rest, axis=-1, keepdims=True)
    i2 = jnp.min(jnp.where(rest == m2, lane, LANES), axis=-1, keepdims=True)
    oh2 = lane == i2
    e = jnp.exp(m2 - m1)
    g1 = 1.0 / (1.0 + e)
    g2 = e * g1
    oh = jnp.where(oh1, 1.0, 0.0) + jnp.where(oh2, 1.0, 0.0)
    cnt = jnp.sum(oh, axis=0, keepdims=True)
    run_units = jnp.floor((cnt + (RUN_ALIGN - 1.0)) * (1.0 / RUN_ALIGN))
    units8 = jnp.broadcast_to(run_units, (SUBLANES, LANES)).astype(BF16)
    run_start = _dot(units8, upper_ref[...])[0:1, :] * RUN_ALIGN
    before = _dot(tri_ref[...], oh.astype(BF16)) + run_start
    p1 = jnp.sum(jnp.where(oh1, before, 0.0), axis=-1, keepdims=True)
    p2 = jnp.sum(jnp.where(oh2, before, 0.0), axis=-1, keepdims=True)
    info = jnp.where(lane == 0, p1,
           jnp.where(lane == 1, p2,
           jnp.where(lane == 2, g1,
           jnp.where(lane == 3, g2, 0.0))))
    info_ref[...] = info
    sub = lax.broadcasted_iota(jnp.int32, (SUBLANES, LANES), 0)
    runs_ref[0] = jnp.where(sub == 0, run_units * RUN_ALIGN, jnp.where(sub == 1, run_start, 0.0))


def _router(x_all, mod, g, w_router, *, n_ctx, seq):
    n, d = x_all.shape
    n_experts = w_router.shape[1]
    tm = TOKEN_TILE
    wr = jnp.pad(w_router, ((0, 0), (0, LANES - n_experts)))
    w_hi = wr.astype(BF16)
    wr = jnp.concatenate([w_hi, (wr - w_hi.astype(F32)).astype(BF16)], axis=1)
    tri = jnp.asarray(np.tril(np.ones((tm, tm), np.float32), -1), BF16)
    upper = jnp.asarray(np.triu(np.ones((LANES, LANES), np.float32), 1), BF16)
    row = lambda i: (i, 0)
    return pl.pallas_call(
        functools.partial(_router_kernel, n_experts=n_experts),
        out_shape=[jax.ShapeDtypeStruct((n, d), BF16), jax.ShapeDtypeStruct((n, LANES), F32),
                   jax.ShapeDtypeStruct((n // tm, SUBLANES, LANES), F32)],
        grid=(n // tm,),
        in_specs=[pl.BlockSpec((tm, d), row), pl.BlockSpec((1, 6, d), _seg_index_map(tm, n_ctx, seq)),
                  _const_spec((1, d)), _const_spec(wr.shape), _const_spec(tri.shape), _const_spec(upper.shape)],
        out_specs=[pl.BlockSpec((tm, d), row), pl.BlockSpec((tm, LANES), row),
                   pl.BlockSpec((1, SUBLANES, LANES), lambda i: (i, 0, 0))],
        compiler_params=_cparams(("parallel",)),
        name="moe_router",
    )(x_all, mod, g, wr, tri, upper)


def _run_copies(n_experts, len_ref, start_ref, dst_ref, tile, make_copy):
    ops = []
    for e in range(n_experts):
        n = len_ref[tile * n_experts + e]
        src = start_ref[tile * n_experts + e]
        dst = dst_ref[tile * n_experts + e]
        ops.append((n > 0, make_copy(pl.multiple_of(src, RUN_ALIGN), pl.multiple_of(dst, RUN_ALIGN),
                                     pl.multiple_of(n, RUN_ALIGN))))
    return ops


def _dispatch_kernel(zrow_ref, len_ref, start_ref, dst_ref, h_ref, info_ref, xg_ref, zero_scr, xs_scr, sem, zsem,
                     *, n_zero, n_experts):
    tm = h_ref.shape[0]
    tz = zero_scr.shape[0]
    rows = xs_scr.shape[1]

    def zero_copy(e):
        return pltpu.make_async_copy(zero_scr, xg_ref.at[pl.ds(pl.multiple_of(zrow_ref[e], tz), tz)], zsem)

    @pl.when(pl.program_id(0) == 0)
    def _():
        zero_scr[...] = jnp.zeros_like(zero_scr)
        for e in range(n_zero):
            @pl.when(zrow_ref[e] >= 0)
            def _():
                zero_copy(e).start()
        for e in range(n_zero):
            @pl.when(zrow_ref[e] >= 0)
            def _():
                zero_copy(e).wait()

    info_t = info_ref[...].T
    slot1 = info_t[0:1, :].astype(jnp.int32)
    slot2 = info_t[1:2, :].astype(jnp.int32)
    r = lax.broadcasted_iota(jnp.int32, (rows, tm), 0)
    perm = (jnp.where(r == slot1, 1.0, 0.0) + jnp.where(r == slot2, 1.0, 0.0)).astype(BF16)
    step = pl.program_id(0)
    buf = step % 2
    xs_scr[buf] = _dot(perm, h_ref[...])

    def run_writes(tile, b):
        def make_copy(src, dst, size):
            return pltpu.make_async_copy(xs_scr.at[b, pl.ds(src, size)], xg_ref.at[pl.ds(dst, size)], sem.at[b])
        return _run_copies(n_experts, len_ref, start_ref, dst_ref, tile, make_copy)

    for cond, copy in run_writes(step, buf):
        pl.when(cond)(copy.start)

    @pl.when(step > 0)
    def _():
        for cond, copy in run_writes(step - 1, 1 - buf):
            pl.when(cond)(copy.wait)

    @pl.when(step == pl.num_programs(0) - 1)
    def _():
        for cond, copy in run_writes(step, buf):
            pl.when(cond)(copy.wait)


def _dispatch(h_all, info, run_len, run_start, run_dst, zero_tile_row, n_rows):
    n, d = h_all.shape
    tm = TOKEN_TILE
    n_zero = zero_tile_row.shape[0]
    n_experts = run_len.shape[0] // (n // tm)
    grid_spec = pltpu.PrefetchScalarGridSpec(
        num_scalar_prefetch=4,
        grid=(n // tm,),
        in_specs=[pl.BlockSpec((tm, d), lambda i, *_: (i, 0)),
                  pl.BlockSpec((tm, LANES), lambda i, *_: (i, 0))],
        out_specs=pl.BlockSpec(memory_space=pl.ANY),
        scratch_shapes=[pltpu.VMEM((MOE_TILE, d), F32), pltpu.VMEM((2, _local_rows(tm, n_experts), d), F32),
                        pltpu.SemaphoreType.DMA((2,)), pltpu.SemaphoreType.DMA(())],
    )
    return pl.pallas_call(
        functools.partial(_dispatch_kernel, n_zero=n_zero, n_experts=n_experts),
        out_shape=jax.ShapeDtypeStruct((n_rows, d), F32),
        grid_spec=grid_spec,
        compiler_params=_cparams(("arbitrary",)),
        name="moe_dispatch",
    )(zero_tile_row, run_len, run_start, run_dst, h_all, info)


def _experts_kernel(te_ref, nt_ref, x_ref, w1_ref, w3_ref, w2_ref, o_ref, xb_scr):
    c = pl.program_id(1)

    @pl.when(pl.program_id(0) < nt_ref[0])
    def _():
        @pl.when(c == 0)
        def _():
            xb_scr[...] = x_ref[...].astype(BF16)
        xb = xb_scr[...]
        a = (_silu(_dot(xb, w1_ref[0])) * _dot(xb, w3_ref[0])).astype(BF16)
        y = _dot(a, w2_ref[0])

        @pl.when(c == 0)
        def _():
            o_ref[...] = y

        @pl.when(c > 0)
        def _():
            o_ref[...] += y

    @pl.when(jnp.logical_and(pl.program_id(0) >= nt_ref[0], c == 0))
    def _():
        o_ref[...] = jnp.zeros_like(o_ref)


def _experts(xg, w1, w3, w2, tile_expert, n_tiles_used):
    p, d = xg.shape
    tm = MOE_TILE
    ff = w1.shape[2]
    fc = MOE_FF_CHUNK if ff % MOE_FF_CHUNK == 0 else ff
    nc = ff // fc
    n_tiles = p // tm
    wmode = {}

    def tile(t, nt):
        return jnp.minimum(t, nt[0] - 1)

    def chunk(t, c, nt):
        return jnp.where(t < nt[0], c, nc - 1)

    grid_spec = pltpu.PrefetchScalarGridSpec(
        num_scalar_prefetch=2,
        grid=(n_tiles, nc),
        in_specs=[pl.BlockSpec((tm, d), lambda t, c, te, nt: (tile(t, nt), 0)),
                  pl.BlockSpec((1, d, fc), lambda t, c, te, nt: (te[tile(t, nt)], 0, chunk(t, c, nt)), **wmode),
                  pl.BlockSpec((1, d, fc), lambda t, c, te, nt: (te[tile(t, nt)], 0, chunk(t, c, nt)), **wmode),
                  pl.BlockSpec((1, fc, d), lambda t, c, te, nt: (te[tile(t, nt)], chunk(t, c, nt), 0), **wmode)],
        out_specs=pl.BlockSpec((tm, d), lambda t, c, te, nt: (t, 0)),
        scratch_shapes=[pltpu.VMEM((tm, d), BF16)],
    )
    return pl.pallas_call(
        _experts_kernel,
        out_shape=jax.ShapeDtypeStruct((p, d), F32),
        grid_spec=grid_spec,
        compiler_params=_cparams(("arbitrary", "arbitrary"), vmem=MOE_VMEM_LIMIT),
        name="moe_experts",
    )(tile_expert, n_tiles_used, xg, w1, w3, w2)


def _combine_kernel(len_ref, start_ref, dst_ref, yg_ref, x_ref, info_ref, mod_ref, fg_ref, o_ref, ys_scr, sem,
                    *, n_experts, tile0):
    tm = x_ref.shape[0]
    rows = ys_scr.shape[1]
    step = pl.program_id(0)
    buf = step % 2

    def run_reads(tile, b):
        def make_copy(src, dst, size):
            return pltpu.make_async_copy(yg_ref.at[pl.ds(dst, size)], ys_scr.at[b, pl.ds(src, size)], sem.at[b])
        return _run_copies(n_experts, len_ref, start_ref, dst_ref, tile0 + tile, make_copy)

    @pl.when(step == 0)
    def _():
        ys_scr[...] = jnp.zeros_like(ys_scr)
        for cond, copy in run_reads(step, buf):
            pl.when(cond)(copy.start)

    @pl.when(step + 1 < pl.num_programs(0))
    def _():
        for cond, copy in run_reads(step + 1, 1 - buf):
            pl.when(cond)(copy.start)

    for cond, copy in run_reads(step, buf):
        pl.when(cond)(copy.wait)
    info = info_ref[...]
    ys = ys_scr[buf].astype(BF16)
    lane = lax.broadcasted_iota(jnp.int32, (tm, rows), 1)
    pick1 = jnp.where(lane == info[:, 0:1].astype(jnp.int32), 1.0, 0.0).astype(BF16)
    pick2 = jnp.where(lane == info[:, 1:2].astype(jnp.int32), 1.0, 0.0).astype(BF16)
    y = info[:, 2:3] * _dot(pick1, ys) + info[:, 3:4] * _dot(pick2, ys)
    x = x_ref[...] + mod_ref[0][5:6] * y
    o_ref[...] = x * lax.rsqrt(jnp.mean(x * x, axis=-1, keepdims=True) + NORM_EPS) * fg_ref[...]


def _combine(yg, x_all, info, run_len, run_start, run_dst, mod, final_g, *, row0, n_rows, seq, mod_row0):
    n, d = x_all.shape
    tm = TOKEN_TILE
    t0 = row0 // tm
    per_seq = seq // tm
    n_experts = run_len.shape[0] // (n // tm)
    grid_spec = pltpu.PrefetchScalarGridSpec(
        num_scalar_prefetch=3,
        grid=(n_rows // tm,),
        in_specs=[pl.BlockSpec(memory_space=pl.ANY),
                  pl.BlockSpec((tm, d), lambda i, *_: (t0 + i, 0)),
                  pl.BlockSpec((tm, LANES), lambda i, *_: (t0 + i, 0)),
                  pl.BlockSpec((1, 6, d), lambda i, *_: (mod_row0 + (i // per_seq if mod_row0 else 0), 0, 0)),
                  pl.BlockSpec((1, d), lambda i, *_: (0, 0))],
        out_specs=pl.BlockSpec((tm, d), lambda i, *_: (i, 0)),
        scratch_shapes=[pltpu.VMEM((2, _local_rows(tm, n_experts), d), F32), pltpu.SemaphoreType.DMA((2,))],
    )
    return pl.pallas_call(
        functools.partial(_combine_kernel, n_experts=n_experts, tile0=t0),
        out_shape=jax.ShapeDtypeStruct((n_rows, d), F32),
        grid_spec=grid_spec,
        compiler_params=_cparams(("arbitrary",)),
        name="moe_combine",
    )(run_len, run_start, run_dst, yg, x_all, info, mod, final_g)


def _moe(x_all, mod, g, w_router, w1, w3, w2, final_g, *, n_ctx, seq_ctx_total, seq_lat):
    n, d = x_all.shape
    n_experts = w_router.shape[1]
    tm = TOKEN_TILE
    h_all, info, runs = _router(x_all, mod, g, w_router, n_ctx=n_ctx, seq=seq_lat)
    run_len = runs[:, 0, :n_experts].astype(jnp.int32)
    run_start = runs[:, 1, :n_experts].astype(jnp.int32)
    cnt = jnp.sum(run_len, axis=0)
    padded = ((cnt + MOE_TILE - 1) // MOE_TILE) * MOE_TILE
    ends = jnp.cumsum(padded)
    offs = ends - padded
    run_dst = offs[None, :] + jnp.cumsum(run_len, axis=0) - run_len
    n_tiles = -(-(TOP_K * n + (n // tm) * n_experts * (RUN_ALIGN - 1)) // MOE_TILE) + n_experts
    n_used = (ends[-1] // MOE_TILE).astype(jnp.int32)
    tile_start = jnp.arange(n_tiles, dtype=jnp.int32) * MOE_TILE
    tile_expert = jnp.minimum(jnp.sum(tile_start[:, None] >= ends[None, :], axis=1), n_experts - 1).astype(jnp.int32)
    tile_expert = jnp.where(tile_start < ends[-1], tile_expert, tile_expert[jnp.maximum(n_used - 1, 0)])
    tail_start = tile_start[(TOP_K * n) // MOE_TILE:]
    zero_tile_row = jnp.concatenate([jnp.where(padded > 0, ends - MOE_TILE, -1),
                                     jnp.where(tail_start >= ends[-1], tail_start, -1)]).astype(jnp.int32)
    run_args = (run_len.reshape(-1), run_start.reshape(-1), run_dst.reshape(-1).astype(jnp.int32))
    xg = _dispatch(h_all, info, *run_args, zero_tile_row, n_tiles * MOE_TILE)
    yg = _experts(xg, w1, w3, w2, tile_expert, n_used.reshape(1))
    y_ctx = _combine(yg, x_all, info, *run_args, mod, final_g, row0=0, n_rows=n_ctx, seq=seq_ctx_total, mod_row0=0)
    y_lat = _combine(yg, x_all, info, *run_args, mod, final_g, row0=n_ctx, n_rows=n - n_ctx, seq=seq_lat,
                     mod_row0=1)
    return y_ctx, y_lat


def _rope_tables(seq, hd):
    axis_dim = hd // 2
    nf = axis_dim // 2
    inv = jnp.power(ROPE_BASE, -jnp.arange(nf, dtype=F32) / nf)
    pos = np.arange(seq)
    row = jnp.asarray(pos // GRID_W, F32)
    col = jnp.asarray(pos % GRID_W, F32)
    lane = np.arange(2 * hd)
    dd = lane % hd
    use_col = (dd // axis_dim) == 1
    e = dd % axis_dim
    first = jnp.asarray(e < nf)[None, :]
    freq = inv[e % nf]
    ang = jnp.where(jnp.asarray(use_col)[None, :], col[:, None], row[:, None]) * freq[None, :]
    cos, sin = jnp.cos(ang), jnp.sin(ang)
    return cos, jnp.where(first, -sin, 0.0), jnp.where(first, 0.0, sin)


def _channel_dft(fw):
    g = FOURIER_GROUP_DIM
    ang = (2.0 * np.pi / g) * ((np.arange(g)[:, None] * np.arange(g)[None, :]) % g)
    eye = np.eye(fw // g)
    table = np.concatenate([np.kron(eye, np.cos(ang)), np.kron(eye, np.sin(ang))], axis=1)
    return jnp.asarray(table, F32).astype(BF16)


def kernel(x_prompt, x_sample, cache_k, cache_v, c, c_ctx, w_ada, b_ada, norm_g, final_g, w_in_ab, w_out_ab,
           diff_lambda, diff_subln, conv_w_pw1, conv_w_dw, conv_b_dw, conv_ln_g, conv_ln_b, conv_w_pw2,
           ffn_w1, ffn_w3, ffn_w2, moe_router, moe_w1, moe_w3, moe_w2):
    bp, lp, d = x_prompt.shape
    bs, ls, _ = x_sample.shape
    lc = cache_k.shape[2]
    n_ctx, n_lat = bp * lp, bs * ls
    n_all = n_ctx + n_lat
    dw = N_HEADS * cache_k.shape[-1]
    hd = cache_k.shape[-1] // 2
    fw = w_in_ab.shape[2] - 3 * dw

    rows = -(-(1 + bs) // SUBLANES) * SUBLANES
    cond = jnp.zeros((rows, d), F32).at[0].set(c_ctx).at[1:1 + bs].set(c)
    mod = _modulation(cond, w_ada, b_ada)

    lam_init = 0.8 - 0.6 * math.exp(-0.3 * 0)
    q_scale = (hd ** -0.5) * LOG2E
    w_in = w_in_ab[0].astype(BF16)
    bd = _channel_dft(fw)
    g00 = norm_g[0, 0].reshape(1, d)
    xp = x_prompt.reshape(n_ctx, d)
    xs = x_sample.reshape(n_lat, d)
    acas_p, q_p, k_p, v_p, kc_new, vc_new = _inproj(
        xp, mod[0], g00, w_in, bd, seq=lp, mod_row0=0, rope_tables=None, cache=True, q_scale=q_scale)
    acas_s, q_s, k_s, v_s = _inproj(
        xs, mod[0], g00, w_in, bd, seq=ls, mod_row0=1, rope_tables=_rope_tables(ls, hd), cache=False, q_scale=q_scale)
    four_p = _fourier_short(acas_p, lp)
    four_s = _fourier_long(acas_s, ls)
    lam_p = diff_lambda[0]
    sub_g = diff_subln[0].reshape(1, 2 * hd)
    attn_p = _attention(lam_p, sub_g, q_p, k_p, v_p, None, None, seq=lp, lam_init=lam_init)
    k_ctx = cache_k[:, 0].reshape(bs * lc, dw).astype(BF16)
    v_ctx = cache_v[:, 0].reshape(bs * lc, dw).astype(BF16)
    attn_s = _attention(lam_p, sub_g, q_s, k_s, v_s, k_ctx, v_ctx, seq=ls, lam_init=lam_init)
    w_out = w_out_ab[0].astype(BF16)
    x_all = _outproj_ffn((xp, four_p, attn_p), (xs, four_s, attn_s), w_out, mod[0], norm_g[0, 1].reshape(1, d),
                         ffn_w1[0].astype(BF16), ffn_w3[0].astype(BF16), ffn_w2[0].astype(BF16), seq=ls)

    x_all = _conv(x_all, mod[1], norm_g[1, 0].reshape(1, d), conv_w_pw1[0].astype(BF16), conv_w_dw[0],
                  conv_b_dw[0].reshape(1, d), conv_ln_g[0].reshape(1, d), conv_ln_b[0].reshape(1, d),
                  conv_w_pw2[0].astype(BF16), n_ctx=n_ctx, seq_ctx=lp, seq_lat=ls)
    y_ctx, y_lat = _moe(x_all, mod[1], norm_g[1, 1].reshape(1, d), moe_router[0], moe_w1[0].astype(BF16),
                        moe_w3[0].astype(BF16), moe_w2[0].astype(BF16), final_g.reshape(1, d),
                        n_ctx=n_ctx, seq_ctx_total=n_ctx, seq_lat=ls)

    n_even = w_in_ab.shape[0]
    new_k = kc_new.reshape(bp, n_even, lp, N_HEADS, 2 * hd)
    new_v = vc_new.reshape(bp, n_even, lp, N_HEADS, 2 * hd)
    return (y_ctx.reshape(bp, lp, d), y_lat.reshape(bs, ls, d), new_k, new_v)
```
